```python
import jax, jax.numpy as jnp
from jax import lax
import numpy as np

D_MODEL = 4096
BATCH = 1
SEQ = 8192
DEPTH = 4
DEC_BATCH = 8
DEC_SEQ = 16
PAST_LEN = 4096

CHUNK = 64
Q_BLOCK = 128
CONV_W = 3
A_WIDTH = D_MODEL // 2
N_HEADS = D_MODEL // 128
Q_LORA = D_MODEL // 4
KV_LORA = 512
NOPE_DIM = 128
ROPE_DIM = 64
V_DIM = 128
QK_DIM = NOPE_DIM + ROPE_DIM
ROPE_BASE = 10000.0
MEM_LEN = 256
MEM_HEADS = 4
MEM_HEAD_DIM = 256
D_FF = 11008
EPS = 1e-6
IN_COLS = 3 * A_WIDTH + Q_LORA + KV_LORA + ROPE_DIM + 2 * D_MODEL
IN_SPLITS = (A_WIDTH, 2 * A_WIDTH, 3 * A_WIDTH, 3 * A_WIDTH + Q_LORA,
             3 * A_WIDTH + Q_LORA + KV_LORA, 3 * A_WIDTH + Q_LORA + KV_LORA + ROPE_DIM,
             3 * A_WIDTH + Q_LORA + KV_LORA + ROPE_DIM + D_MODEL)

kernel_name = 'hybrid_streaming_encoder_step'


def rmsnorm(x, g):
    xf = x.astype(jnp.float32)
    y = xf * lax.rsqrt(jnp.mean(xf * xf, axis=-1, keepdims=True) + EPS)
    return (y * g.astype(jnp.float32)).astype(x.dtype)


def rope(x, pos):
    half = ROPE_DIM // 2
    inv_freq = ROPE_BASE ** (-jnp.arange(half, dtype=jnp.float32) / half)
    ang = pos.astype(jnp.float32)[:, None] * inv_freq[None, :]
    cos = jnp.cos(ang)[None, :, None, :]
    sin = jnp.sin(ang)[None, :, None, :]
    xf = x.astype(jnp.float32)
    x1, x2 = xf[..., :half], xf[..., half:]
    return jnp.concatenate([x1 * cos - x2 * sin, x1 * sin + x2 * cos], axis=-1).astype(x.dtype)


def chunk_mask(q_pos, k_pos):
    return (k_pos // CHUNK)[None, :] <= (q_pos // CHUNK)[:, None]


def causal_conv3(u, prev, w):
    T = u.shape[1]
    full = jnp.concatenate([prev.astype(u.dtype), u], axis=1)
    y = w[0] * full[:, 0:T]
    for j in range(1, CONV_W):
        y = y + w[j] * full[:, j:j + T]
    return y, full[:, -(CONV_W - 1):]


def prompt_attention(q, k, v, pos):
    B, S, H, Dk = q.shape
    nb = S // Q_BLOCK
    qb = q.reshape(B, nb, Q_BLOCK, H, Dk).transpose(1, 0, 2, 3, 4)
    pb = pos.reshape(nb, Q_BLOCK)
    scale = QK_DIM ** -0.5

    def one_block(args):
        qi, pi = args
        s = jnp.einsum('bqhd,bkhd->bhqk', qi, k).astype(jnp.float32) * scale
        s = jnp.where(chunk_mask(pi, pos)[None, None], s, -jnp.inf)
        p = jax.nn.softmax(s, axis=-1).astype(v.dtype)
        return jnp.einsum('bhqk,bkhd->bqhd', p, v)

    o = lax.map(one_block, (qb, pb))
    return o.transpose(1, 0, 2, 3, 4).reshape(B, S, H, V_DIM)


def sample_attention(q_nope, q_rope, ckv_all, kr_all, w_uk, w_uv, q_pos, k_pos):
    q_lat = jnp.einsum('bthn,chn->bthc', q_nope, w_uk)
    s = (jnp.einsum('bthc,bkc->bhtk', q_lat, ckv_all)
         + jnp.einsum('bthr,bkr->bhtk', q_rope, kr_all)).astype(jnp.float32) * (QK_DIM ** -0.5)
    s = jnp.where(chunk_mask(q_pos, k_pos)[None, None], s, -jnp.inf)
    p = jax.nn.softmax(s, axis=-1).astype(ckv_all.dtype)
    o_lat = jnp.einsum('bhtk,bkc->bthc', p, ckv_all)
    return jnp.einsum('bthc,chv->bthv', o_lat, w_uv)


def trunk_layer(x, pos, conv_a_prev, ffn_prev, past_ckv, past_kr, mem_k, mem_v,
                norm_mix, w_in, q_norm, kv_norm, w_uq, w_ukv, conv_a, w_out_a, w_o_mla, w_o,
                norm_xattn, w_mq, w_mo, norm_ffn, w_up, conv_ffn, w_down):
    B, T, _ = x.shape
    xn = rmsnorm(x, norm_mix)
    proj = xn @ w_in
    b_a, c_a, h_a, q_lat, kv_lat, k_r, g_a, g_b = jnp.split(proj, IN_SPLITS, axis=-1)
    conv_out, conv_a_new = causal_conv3(c_a * h_a, conv_a_prev, conv_a)
    y_a = (b_a * conv_out) @ w_out_a
    cq = rmsnorm(q_lat, q_norm)
    q = (cq @ w_uq).reshape(B, T, N_HEADS, QK_DIM)
    q_nope = q[..., :NOPE_DIM]
    q_rope = rope(q[..., NOPE_DIM:], pos)
    ckv = rmsnorm(kv_lat, kv_norm)
    kr = rope(k_r[:, :, None, :], pos)[:, :, 0, :]
    w_uk = w_ukv[..., :NOPE_DIM]
    w_uv = w_ukv[..., NOPE_DIM:]
    if past_ckv is None:
        k_nope = jnp.einsum('btc,chn->bthn', ckv, w_uk)
        v = jnp.einsum('btc,chv->bthv', ckv, w_uv)
        k = jnp.concatenate([k_nope, jnp.broadcast_to(kr[:, :, None, :], (B, T, N_HEADS, ROPE_DIM))], axis=-1)
        o_b = prompt_attention(jnp.concatenate([q_nope, q_rope], axis=-1), k, v, pos)
    else:
        P = past_ckv.shape[1]
        ckv_all = jnp.concatenate([past_ckv.astype(ckv.dtype), ckv], axis=1)
        kr_all = jnp.concatenate([past_kr.astype(kr.dtype), kr], axis=1)
        k_pos = jnp.arange(P + T, dtype=jnp.int32)
        o_b = sample_attention(q_nope, q_rope, ckv_all, kr_all, w_uk, w_uv, pos, k_pos)
    y_b = o_b.reshape(B, T, N_HEADS * V_DIM) @ w_o_mla
    x = x + (jax.nn.sigmoid(g_a) * y_a + jax.nn.sigmoid(g_b) * y_b) @ w_o
    xc = rmsnorm(x, norm_xattn)
    qm = (xc @ w_mq).reshape(B, T, MEM_HEADS, MEM_HEAD_DIM)
    sm = jnp.einsum('bthd,bmhd->bhtm', qm, mem_k).astype(jnp.float32) * (MEM_HEAD_DIM ** -0.5)
    pm = jax.nn.softmax(sm, axis=-1).astype(mem_v.dtype)
    om = jnp.einsum('bhtm,bmhd->bthd', pm, mem_v).reshape(B, T, MEM_HEADS * MEM_HEAD_DIM)
    x = x + om @ w_mo
    xf = rmsnorm(x, norm_ffn)
    up_c, ffn_new = causal_conv3(xf @ w_up, ffn_prev, conv_ffn)
    a, g = jnp.split(up_c, 2, axis=-1)
    x = x + (jax.nn.silu(g) * a) @ w_down
    return x, ckv, kr, conv_a_new, ffn_new


def setup_inputs(seed: int = 0) -> dict:
    key = jax.random.key(seed)
    ks = jax.random.split(key, 32)
    f32 = jnp.float32

    def nrm(k, shape, scale=1.0):
        return jax.random.normal(k, shape, f32) * scale

    def gain(k, shape):
        return 1.0 + 0.01 * jax.random.normal(k, shape, f32)

    return {
        'x_prompt': nrm(ks[0], (BATCH, SEQ, D_MODEL)),
        'x_sample': nrm(ks[1], (DEC_BATCH, DEC_SEQ, D_MODEL)),
        'cache_mla_ckv': nrm(ks[2], (DEPTH, DEC_BATCH, PAST_LEN, KV_LORA)),
        'cache_mla_krope': nrm(ks[3], (DEPTH, DEC_BATCH, PAST_LEN, ROPE_DIM)),
        'cache_mem_k': nrm(ks[4], (DEPTH, DEC_BATCH, MEM_LEN, MEM_HEADS, MEM_HEAD_DIM)),
        'cache_mem_v': nrm(ks[5], (DEPTH, DEC_BATCH, MEM_LEN, MEM_HEADS, MEM_HEAD_DIM)),
        'state_conv_a': nrm(ks[6], (DEPTH, DEC_BATCH, CONV_W - 1, A_WIDTH)),
        'state_conv_ffn': nrm(ks[7], (DEPTH, DEC_BATCH, CONV_W - 1, 2 * D_FF)),
        'mem_prompt': nrm(ks[8], (BATCH, MEM_LEN, D_MODEL)),
        'norm_mix': gain(ks[9], (DEPTH, D_MODEL)),
        'w_in': nrm(ks[10], (DEPTH, D_MODEL, IN_COLS), D_MODEL ** -0.5),
        'q_norm': gain(ks[11], (DEPTH, Q_LORA)),
        'kv_norm': gain(ks[12], (DEPTH, KV_LORA)),
        'w_uq': nrm(ks[13], (DEPTH, Q_LORA, N_HEADS * QK_DIM), Q_LORA ** -0.5),
        'w_ukv': nrm(ks[14], (DEPTH, KV_LORA, N_HEADS, NOPE_DIM + V_DIM), KV_LORA ** -0.5),
        'conv_a': nrm(ks[15], (DEPTH, CONV_W, A_WIDTH), CONV_W ** -0.5),
        'w_out_a': nrm(ks[16], (DEPTH, A_WIDTH, D_MODEL), A_WIDTH ** -0.5),
        'w_o_mla': nrm(ks[17], (DEPTH, N_HEADS * V_DIM, D_MODEL), (N_HEADS * V_DIM) ** -0.5),
        'w_o': nrm(ks[18], (DEPTH, D_MODEL, D_MODEL), D_MODEL ** -0.5),
        'norm_xattn': gain(ks[19], (DEPTH, D_MODEL)),
        'mem_norm': gain(ks[20], (DEPTH, D_MODEL)),
        'w_mq': nrm(ks[21], (DEPTH, D_MODEL, MEM_HEADS * MEM_HEAD_DIM), D_MODEL ** -0.5),
        'w_mk': nrm(ks[22], (DEPTH, D_MODEL, MEM_HEADS * MEM_HEAD_DIM), D_MODEL ** -0.5),
        'w_mv': nrm(ks[23], (DEPTH, D_MODEL, MEM_HEADS * MEM_HEAD_DIM), D_MODEL ** -0.5),
        'w_mo': nrm(ks[24], (DEPTH, MEM_HEADS * MEM_HEAD_DIM, D_MODEL), (MEM_HEADS * MEM_HEAD_DIM) ** -0.5),
        'norm_ffn': gain(ks[25], (DEPTH, D_MODEL)),
        'w_up': nrm(ks[26], (DEPTH, D_MODEL, 2 * D_FF), D_MODEL ** -0.5),
        'conv_ffn': nrm(ks[27], (DEPTH, CONV_W, 2 * D_FF), CONV_W ** -0.5),
        'w_down': nrm(ks[28], (DEPTH, D_FF, D_MODEL), D_FF ** -0.5),
        'norm_final': gain(ks[29], (D_MODEL,)),
    }


def reference(x_prompt, x_sample, cache_mla_ckv, cache_mla_krope, cache_mem_k, cache_mem_v,
              state_conv_a, state_conv_ffn, mem_prompt,
              norm_mix, w_in, q_norm, kv_norm, w_uq, w_ukv, conv_a, w_out_a, w_o_mla, w_o,
              norm_xattn, mem_norm, w_mq, w_mk, w_mv, w_mo, norm_ffn, w_up, conv_ffn, w_down,
              norm_final):
    Bp, S, _ = x_prompt.shape
    Bs, T, _ = x_sample.shape
    P = cache_mla_ckv.shape[2]
    pos_p = jnp.arange(S, dtype=jnp.int32)
    pos_s = P + jnp.arange(T, dtype=jnp.int32)
    zeros_a = jnp.zeros((Bp, CONV_W - 1, A_WIDTH), x_prompt.dtype)
    zeros_f = jnp.zeros((Bp, CONV_W - 1, 2 * D_FF), x_prompt.dtype)
    hp, hs = x_prompt, x_sample
    ckv_p_l, kr_p_l, mk_p_l, mv_p_l, ca_p_l, cf_p_l = [], [], [], [], [], []
    ckv_s_l, kr_s_l, ca_s_l, cf_s_l = [], [], [], []
    for l in range(DEPTH):
        lw = (norm_mix[l], w_in[l], q_norm[l], kv_norm[l], w_uq[l], w_ukv[l], conv_a[l],
              w_out_a[l], w_o_mla[l], w_o[l], norm_xattn[l], w_mq[l], w_mo[l], norm_ffn[l],
              w_up[l], conv_ffn[l], w_down[l])
        mn = rmsnorm(mem_prompt, mem_norm[l])
        mk = (mn @ w_mk[l]).reshape(Bp, MEM_LEN, MEM_HEADS, MEM_HEAD_DIM)
        mv = (mn @ w_mv[l]).reshape(Bp, MEM_LEN, MEM_HEADS, MEM_HEAD_DIM)
        hp, ckv_p, kr_p, ca_p, cf_p = trunk_layer(hp, pos_p, zeros_a, zeros_f, None, None, mk, mv, *lw)
        hs, ckv_s, kr_s, ca_s, cf_s = trunk_layer(hs, pos_s, state_conv_a[l], state_conv_ffn[l],
                                                  cache_mla_ckv[l], cache_mla_krope[l],
                                                  cache_mem_k[l], cache_mem_v[l], *lw)
        ckv_p_l.append(ckv_p); kr_p_l.append(kr_p); mk_p_l.append(mk); mv_p_l.append(mv)
        ca_p_l.append(ca_p); cf_p_l.append(cf_p)
        ckv_s_l.append(ckv_s); kr_s_l.append(kr_s); ca_s_l.append(ca_s); cf_s_l.append(cf_s)
    y_prompt = rmsnorm(hp, norm_final)
    y_sample = rmsnorm(hs, norm_final)
    return (y_prompt, y_sample,
            jnp.stack(ckv_p_l), jnp.stack(kr_p_l), jnp.stack(mk_p_l), jnp.stack(mv_p_l),
            jnp.stack(ca_p_l), jnp.stack(cf_p_l),
            jnp.stack(ckv_s_l), jnp.stack(kr_s_l), jnp.stack(ca_s_l), jnp.stack(cf_s_l))
```

```python
import functools

import jax
import jax.numpy as jnp
from jax import lax
from jax.experimental import pallas as pl
from jax.experimental.pallas import tpu as pltpu

F32 = jnp.float32
BF16 = jnp.bfloat16

EPS = 1e-6
CHUNK = 64
ROPE_BASE = 10000.0
NEG_BIG = -1e30

VMEM_LIMIT_BYTES = 60 * 1024 * 1024
LANE = 128
SUBLANE = 8
HALO_ROWS = 8
MM_ROW_TILE = 1040
EW_ROW_TILE = 256
FA_BLOCK = 512
NT_DIMS = (((1,), (1,)), ((), ()))


def _pick(n, target, mult):
    best = None
    for d in range(mult, min(n, target) + 1, mult):
        if n % d == 0:
            best = d
    return best or n


def _params(*sem):
    return pltpu.CompilerParams(dimension_semantics=sem, vmem_limit_bytes=VMEM_LIMIT_BYTES)


def _rms(x, g):
    return x * lax.rsqrt(jnp.mean(x * x, axis=-1, keepdims=True) + EPS) * g


def _rms_kernel(x_ref, g_ref, o_ref):
    o_ref[...] = _rms(x_ref[...].astype(F32), g_ref[...]).astype(o_ref.dtype)


def rmsnorm_rows(x, g, out_dtype, row0=0, nrows=None, bm_target=512):
    R, C = x.shape
    nrows = R - row0 if nrows is None else nrows
    bm = _pick(nrows, bm_target, 16)
    assert row0 % bm == 0
    off = row0 // bm
    return pl.pallas_call(
        _rms_kernel,
        grid=(nrows // bm,),
        in_specs=[pl.BlockSpec((bm, C), lambda i: (i + off, 0)),
                  pl.BlockSpec((1, C), lambda i: (0, 0))],
        out_specs=pl.BlockSpec((bm, C), lambda i: (i, 0)),
        out_shape=jax.ShapeDtypeStruct((nrows, C), out_dtype),
        compiler_params=_params("parallel"),
        name="rmsnorm",
    )(x, g.reshape(1, C).astype(F32))


def _mm_kernel(a_ref, w_ref, o_ref):
    o_ref[...] = jnp.dot(a_ref[...], w_ref[...], preferred_element_type=F32).astype(o_ref.dtype)


def _mm_res_kernel(a_ref, w_ref, r_ref, o_ref):
    acc = jnp.dot(a_ref[...], w_ref[...], preferred_element_type=F32)
    o_ref[...] = (r_ref[...] + acc).astype(o_ref.dtype)


def matmul(a, w, out_dtype, res=None, bm_target=MM_ROW_TILE, bn_target=1024, name="matmul"):
    M, K = a.shape
    N = w.shape[1]
    bm = _pick(M, bm_target, 16)
    bn = _pick(N, bn_target, LANE)
    in_specs = [pl.BlockSpec((bm, K), lambda i, j: (i, 0)),
                pl.BlockSpec((K, bn), lambda i, j: (0, j))]
    args = [a, w]
    body = _mm_kernel
    if res is not None:
        in_specs.append(pl.BlockSpec((bm, bn), lambda i, j: (i, j)))
        args.append(res)
        body = _mm_res_kernel
    return pl.pallas_call(
        body,
        grid=(M // bm, N // bn),
        in_specs=in_specs,
        out_specs=pl.BlockSpec((bm, bn), lambda i, j: (i, j)),
        out_shape=jax.ShapeDtypeStruct((M, N), out_dtype),
        compiler_params=_params("parallel", "parallel"),
        name=name,
    )(*args)


def _mm_q_kernel(a_ref, w_ref, t_ref, o_ref, *, heads, scale):
    acc = jnp.dot(a_ref[...], w_ref[...], preferred_element_type=F32)
    t = t_ref[...]
    for h in range(heads):
        c0 = 2 * LANE * h
        o_ref[:, c0:c0 + LANE] = (acc[:, c0:c0 + LANE] * scale).astype(o_ref.dtype)
        o_ref[:, c0 + LANE:c0 + 2 * LANE] = (acc[:, c0 + LANE:c0 + 2 * LANE] * t).astype(o_ref.dtype)


def matmul_q(a, w, tq, scale, heads_per_tile=4):
    M, K = a.shape
    N = w.shape[1]
    bm = _pick(M, MM_ROW_TILE, 16)
    bn = 2 * LANE * heads_per_tile
    assert N % bn == 0
    return pl.pallas_call(
        functools.partial(_mm_q_kernel, heads=heads_per_tile, scale=scale),
        grid=(M // bm, N // bn),
        in_specs=[pl.BlockSpec((bm, K), lambda i, j: (i, 0)),
                  pl.BlockSpec((K, bn), lambda i, j: (0, j)),
                  pl.BlockSpec((bm, LANE), lambda i, j: (i, 0))],
        out_specs=pl.BlockSpec((bm, bn), lambda i, j: (i, j)),
        out_shape=jax.ShapeDtypeStruct((M, N), BF16),
        compiler_params=_params("parallel", "parallel"),
        name="matmul_q",
    )(a, w, tq)


def _mm_kv_kernel(a_ref, wk_ref, wv_ref, kr_ref, k_ref, v_ref, *, heads):
    a = a_ref[...]
    kn = jnp.dot(a, wk_ref[...], preferred_element_type=F32)
    kr = kr_ref[...]
    for h in range(heads):
        k_ref[:, 2 * LANE * h:2 * LANE * h + LANE] = kn[:, LANE * h:LANE * (h + 1)].astype(k_ref.dtype)
        k_ref[:, 2 * LANE * h + LANE:2 * LANE * (h + 1)] = kr
    v_ref[...] = jnp.dot(a, wv_ref[...], preferred_element_type=F32).astype(v_ref.dtype)


def matmul_kv(ckv_b, krc_b, w_uk, w_uv, S, heads_per_tile=4):
    C = ckv_b.shape[1]
    N = w_uk.shape[1]
    bm = _pick(S, 1024, 16)
    bn = LANE * heads_per_tile
    assert N % bn == 0
    return pl.pallas_call(
        functools.partial(_mm_kv_kernel, heads=heads_per_tile),
        grid=(S // bm, N // bn),
        in_specs=[pl.BlockSpec((bm, C), lambda i, j: (i, 0)),
                  pl.BlockSpec((C, bn), lambda i, j: (0, j)),
                  pl.BlockSpec((C, bn), lambda i, j: (0, j)),
                  pl.BlockSpec((bm, LANE), lambda i, j: (i, 0))],
        out_specs=[pl.BlockSpec((bm, 2 * bn), lambda i, j: (i, j)),
                   pl.BlockSpec((bm, bn), lambda i, j: (i, j))],
        out_shape=[jax.ShapeDtypeStruct((S, 2 * N), BF16),
                   jax.ShapeDtypeStruct((S, N), BF16)],
        compiler_params=_params("parallel", "parallel"),
        name="matmul_kv",
    )(ckv_b, w_uk, w_uv, krc_b)


def _sigmoid(x):
    return 1.0 / (1.0 + jnp.exp(-x))


def _mm_merge_kernel(a1_ref, w1_ref, a2_ref, w2_ref, ga_ref, gb_ref, o_ref):
    ya = jnp.dot(a1_ref[...], w1_ref[...], preferred_element_type=F32)
    yb = jnp.dot(a2_ref[...], w2_ref[...], preferred_element_type=F32)
    o_ref[...] = (_sigmoid(ga_ref[...]) * ya + _sigmoid(gb_ref[...]) * yb).astype(o_ref.dtype)


def matmul_merge(ya_in, w_out_a, ob, w_o_mla, proj, ga_col0, gb_col0):
    M, K1 = ya_in.shape
    K2 = ob.shape[1]
    N = w_out_a.shape[1]
    bm = _pick(M, MM_ROW_TILE // 2, 16)
    bn = _pick(N, 512, LANE)
    assert ga_col0 % bn == 0 and gb_col0 % bn == 0
    ja, jb = ga_col0 // bn, gb_col0 // bn
    return pl.pallas_call(
        _mm_merge_kernel,
        grid=(M // bm, N // bn),
        in_specs=[pl.BlockSpec((bm, K1), lambda i, j: (i, 0)),
                  pl.BlockSpec((K1, bn), lambda i, j: (0, j)),
                  pl.BlockSpec((bm, K2), lambda i, j: (i, 0)),
                  pl.BlockSpec((K2, bn), lambda i, j: (0, j)),
                  pl.BlockSpec((bm, bn), lambda i, j: (i, j + ja)),
                  pl.BlockSpec((bm, bn), lambda i, j: (i, j + jb))],
        out_specs=pl.BlockSpec((bm, bn), lambda i, j: (i, j)),
        out_shape=jax.ShapeDtypeStruct((M, N), BF16),
        compiler_params=_params("parallel", "parallel"),
        name="matmul_merge",
    )(ya_in, w_out_a, ob, w_o_mla, proj, proj)


def _conv3(u, h0, h1, w):
    row = lax.broadcasted_iota(jnp.int32, (u.shape[0], 1), 0)
    p1 = jnp.where(row == 0, h1, pltpu.roll(u, 1, 0))
    p2 = jnp.where(row == 0, h0, jnp.where(row == 1, h1, pltpu.roll(u, 2, 0)))
    return w[0:1] * p2 + w[1:2] * p1 + w[2:3] * u


def _halo(prev_u, st_ref, first):
    st = st_ref[...]
    if prev_u is None:
        return st[0:1], st[1:2]
    return (jnp.where(first, st[0:1], prev_u[HALO_ROWS - 2:HALO_ROWS - 1]),
            jnp.where(first, st[1:2], prev_u[HALO_ROWS - 1:HALO_ROWS]))


def _postin_kernel(*refs, has_prev):
    if has_prev:
        (b_ref, c_ref, h_ref, cp_ref, hp_ref, ql_ref, kv_ref, kr_ref, st_ref, taps_ref, qn_ref, kvn_ref,
         t_ref, ya_ref, cq_ref, ckv_ref, ckvb_ref, krc_ref, krcb_ref, ut_ref) = refs
        prev_u = cp_ref[...] * hp_ref[...]
    else:
        (b_ref, c_ref, h_ref, ql_ref, kv_ref, kr_ref, st_ref, taps_ref, qn_ref, kvn_ref,
         t_ref, ya_ref, cq_ref, ckv_ref, ckvb_ref, krc_ref, krcb_ref, ut_ref) = refs
        prev_u = None
    u = c_ref[...] * h_ref[...]
    h0, h1 = _halo(prev_u, st_ref, pl.program_id(0) == 0)
    ya_ref[...] = (b_ref[...] * _conv3(u, h0, h1, taps_ref[...])).astype(ya_ref.dtype)
    ut_ref[...] = u[u.shape[0] - HALO_ROWS:, :]
    cq_ref[...] = _rms(ql_ref[...], qn_ref[...]).astype(cq_ref.dtype)
    ckv = _rms(kv_ref[...], kvn_ref[...])
    ckv_ref[...] = ckv
    ckvb_ref[...] = ckv.astype(ckvb_ref.dtype)
    t = kr_ref[...] * t_ref[...]
    krc = t + pltpu.roll(t, LANE // 2, 1)
    krc_ref[...] = krc
    krcb_ref[...] = krc.astype(krcb_ref.dtype)


def post_in(proj, lay, state, taps, q_norm, kv_norm, tk, S, Bs, T, prev_outs=None):
    R = proj.shape[0]
    A, Q, C = lay["A"], lay["Q"], lay["C"]
    prompt = prev_outs is None
    bm = _pick(S, EW_ROW_TILE, 16) if prompt else T
    assert bm % 16 == 0 and S % bm == 0
    off = 0 if prompt else S // bm
    steps = S // bm if prompt else Bs
    rb = bm // HALO_ROWS

    def col(width, col0):
        assert col0 % width == 0
        return pl.BlockSpec((bm, width), lambda i, c=col0 // width: (i + off, c))

    in_specs = [col(A, lay["b"]), col(A, lay["c"]), col(A, lay["h"])]
    args = [proj, proj, proj]
    if prompt:
        for name in ("c", "h"):
            in_specs.append(pl.BlockSpec((HALO_ROWS, A),
                                         lambda i, c=lay[name] // A: (jnp.maximum(i * rb - 1, 0), c)))
            args.append(proj)
    in_specs += [col(Q, lay["q"]), col(C, lay["kv"]), col(LANE, lay["kr"]),
                 pl.BlockSpec((None, 2, A), (lambda i: (0, 0, 0)) if prompt else (lambda i: (i, 0, 0))),
                 pl.BlockSpec((3, A), lambda i: (0, 0)),
                 pl.BlockSpec((1, Q), lambda i: (0, 0)),
                 pl.BlockSpec((1, C), lambda i: (0, 0)),
                 pl.BlockSpec((bm, LANE), lambda i: (i + off, 0))]
    args += [proj, proj, proj, state, taps, q_norm.reshape(1, Q), kv_norm.reshape(1, C), tk]
    out_widths = [(A, BF16), (Q, BF16), (C, F32), (C, BF16), (LANE, F32), (LANE, BF16)]
    out_specs = [pl.BlockSpec((bm, w), lambda i: (i + off, 0)) for w, _ in out_widths]
    out_shape = [jax.ShapeDtypeStruct((R, w), dt) for w, dt in out_widths]
    n_pt = S // _pick(S, EW_ROW_TILE, 16)
    toff = 0 if prompt else n_pt
    out_specs.append(pl.BlockSpec((HALO_ROWS, A), lambda i: (i + toff, 0)))
    out_shape.append(jax.ShapeDtypeStruct(((n_pt + Bs) * HALO_ROWS, A), F32))
    aliases = {}
    if not prompt:
        n_in = len(args)
        for k, arr in enumerate(prev_outs):
            in_specs.append(pl.BlockSpec(memory_space=pl.ANY))
            args.append(arr)
            aliases[n_in + k] = k
    return pl.pallas_call(
        functools.partial(_postin_kernel_aliased if not prompt else _postin_kernel, has_prev=prompt),
        grid=(steps,),
        in_specs=in_specs,
        out_specs=out_specs,
        out_shape=out_shape,
        input_output_aliases=aliases,
        compiler_params=_params("arbitrary"),
        name="post_in_prompt" if prompt else "post_in_sample",
    )(*args)


def _postin_kernel_aliased(*refs, has_prev):
    n_out = 7
    n_in = len(refs) - 2 * n_out
    _postin_kernel(*refs[:n_in], *refs[n_in + n_out:], has_prev=has_prev)


def _ffn_kernel(*refs, has_prev, aliased):
    if aliased:
        refs = refs[:-2] + refs[-1:]
    if has_prev:
        a_ref, g_ref, ap_ref, gp_ref, sa_ref, sg_ref, wa_ref, wg_ref, o_ref = refs
        pa, pg = ap_ref[...], gp_ref[...]
    else:
        a_ref, g_ref, sa_ref, sg_ref, wa_ref, wg_ref, o_ref = refs
        pa = pg = None
    first = pl.program_id(0) == 0
    a0, a1 = _halo(pa, sa_ref, first)
    g0, g1 = _halo(pg, sg_ref, first)
    a = _conv3(a_ref[...], a0, a1, wa_ref[...])
    g = _conv3(g_ref[...], g0, g1, wg_ref[...])
    o_ref[...] = (g * _sigmoid(g) * a).astype(o_ref.dtype)


def ffn_gate(up, state, taps, S, Bs, T, prev_out=None):
    R, F2 = up.shape
    F = F2 // 2
    prompt = prev_out is None
    bm = _pick(S, EW_ROW_TILE, 16) if prompt else T
    bc = _pick(F, 5504, LANE)
    nc = F // bc
    off = 0 if prompt else S // bm
    steps = S // bm if prompt else Bs
    rb = bm // HALO_ROWS
    in_specs = [pl.BlockSpec((bm, bc), lambda i, j: (i + off, j)),
                pl.BlockSpec((bm, bc), lambda i, j: (i + off, j + nc))]
    args = [up, up]
    if prompt:
        in_specs += [pl.BlockSpec((HALO_ROWS, bc), lambda i, j: (jnp.maximum(i * rb - 1, 0), j)),
                     pl.BlockSpec((HALO_ROWS, bc), lambda i, j: (jnp.maximum(i * rb - 1, 0), j + nc))]
        args += [up, up]
    bsel = (lambda i: 0) if prompt else (lambda i: i)
    in_specs += [pl.BlockSpec((None, 2, bc), lambda i, j: (bsel(i), 0, j)),
                 pl.BlockSpec((None, 2, bc), lambda i, j: (bsel(i), 0, j + nc)),
                 pl.BlockSpec((3, bc), lambda i, j: (0, j)),
                 pl.BlockSpec((3, bc), lambda i, j: (0, j + nc))]
    args += [state, state, taps, taps]
    aliases = {}
    if not prompt:
        in_specs.append(pl.BlockSpec(memory_space=pl.ANY))
        args.append(prev_out)
        aliases[len(args) - 1] = 0
    return pl.pallas_call(
        functools.partial(_ffn_kernel, has_prev=prompt, aliased=not prompt),
        grid=(steps, nc),
        in_specs=in_specs,
        out_specs=pl.BlockSpec((bm, bc), lambda i, j: (i + off, j)),
        out_shape=jax.ShapeDtypeStruct((R, F), BF16),
        input_output_aliases=aliases,
        compiler_params=_params("arbitrary", "arbitrary"),
        name="ffn_gate_prompt" if prompt else "ffn_gate_sample",
    )(*args)


def _fa_kernel(q_ref, k_ref, v_ref, o_ref, m_sc, l_sc, acc_sc, *, bq, bk):
    i = pl.program_id(1)
    q = q_ref[...]
    m_sc[...] = jnp.full(m_sc.shape, NEG_BIG, F32)
    l_sc[...] = jnp.zeros(l_sc.shape, F32)
    acc_sc[...] = jnp.zeros(acc_sc.shape, F32)

    def step(kj, vj, mask):
        s = lax.dot_general(q, kj, NT_DIMS, preferred_element_type=F32)
        if mask is not None:
            s = jnp.where(mask, s, NEG_BIG)
        m_prev = m_sc[...]
        m_new = jnp.maximum(m_prev, jnp.max(s, axis=-1, keepdims=True))
        alpha = jnp.exp(m_prev - m_new)
        p = jnp.exp(s - m_new)
        l_sc[...] = alpha * l_sc[...] + jnp.sum(p, axis=-1, keepdims=True)
        acc_sc[...] = alpha * acc_sc[...] + jnp.dot(p.astype(vj.dtype), vj, preferred_element_type=F32)
        m_sc[...] = m_new

    d0 = pl.multiple_of(i * bq, bq)
    row = lax.broadcasted_iota(jnp.int32, (bq, bq), 0) // CHUNK
    colc = lax.broadcasted_iota(jnp.int32, (bq, bq), 1) // CHUNK
    step(k_ref[pl.ds(d0, bq), :], v_ref[pl.ds(d0, bq), :], colc <= row)

    def body(j, carry):
        k0 = pl.multiple_of(j * bk, bk)
        step(k_ref[pl.ds(k0, bk), :], v_ref[pl.ds(k0, bk), :], None)
        return carry

    lax.fori_loop(0, i * (bq // bk), body, 0)
    o_ref[...] = (acc_sc[...] / l_sc[...]).astype(o_ref.dtype)


def prompt_attention(q_pad, k_pad, v, S, R, H):
    bq = _pick(S, FA_BLOCK, CHUNK)
    bk = bq
    dv = v.shape[1] // H
    return pl.pallas_call(
        functools.partial(_fa_kernel, bq=bq, bk=bk),
        grid=(H, S // bq),
        in_specs=[pl.BlockSpec((bq, 2 * LANE), lambda h, i: (i, h)),
                  pl.BlockSpec((S, 2 * LANE), lambda h, i: (0, h)),
                  pl.BlockSpec((S, dv), lambda h, i: (0, h))],
        out_specs=pl.BlockSpec((bq, dv), lambda h, i: (i, h)),
        out_shape=jax.ShapeDtypeStruct((R, H * dv), BF16),
        scratch_shapes=[pltpu.VMEM((bq, 1), F32), pltpu.VMEM((bq, 1), F32), pltpu.VMEM((bq, dv), F32)],
        compiler_params=_params("parallel", "arbitrary"),
        name="prompt_attention",
    )(q_pad, k_pad, v)


def _sattn_kernel(q_ref, wuk_ref, wuv_ref, cckv_ref, ckr_ref, nckv_ref, nkr_ref, ob_ref, o_ref,
                  ql_sc, ka_sc, *, H, T, C, P, KPAD):
    del ob_ref
    for h in range(H):
        qn = q_ref[:, 2 * LANE * h:2 * LANE * h + LANE]
        qlat = lax.dot_general(qn, wuk_ref[:, LANE * h:LANE * (h + 1)], NT_DIMS, preferred_element_type=F32)
        ql_sc[T * h:T * (h + 1), 0:C] = qlat.astype(ql_sc.dtype)
        ql_sc[T * h:T * (h + 1), C:C + LANE] = q_ref[:, 2 * LANE * h + LANE:2 * LANE * (h + 1)]
    ka_sc[0:P, 0:C] = cckv_ref[...].astype(ka_sc.dtype)
    ka_sc[0:P, C:C + LANE] = ckr_ref[...]
    ka_sc[P:P + T, 0:C] = nckv_ref[...]
    ka_sc[P:P + T, C:C + LANE] = nkr_ref[...]
    ka_sc[P + T:P + KPAD, :] = jnp.zeros((KPAD - T, C + LANE), ka_sc.dtype)
    ka = ka_sc[...]
    s = lax.dot_general(ql_sc[...], ka, NT_DIMS, preferred_element_type=F32)
    kpos = lax.broadcasted_iota(jnp.int32, s.shape, 1)
    qpos = P + lax.broadcasted_iota(jnp.int32, s.shape, 0) % T
    s = jnp.where((kpos < P + T) & (kpos // CHUNK <= qpos // CHUNK), s, NEG_BIG)
    m = jnp.max(s, axis=-1, keepdims=True)
    p = jnp.exp(s - m)
    l = jnp.sum(p, axis=-1, keepdims=True)
    olat = (jnp.dot(p.astype(ka.dtype), ka[:, 0:C], preferred_element_type=F32) / l).astype(wuv_ref.dtype)
    for h in range(H):
        oh = jnp.dot(olat[T * h:T * (h + 1), :], wuv_ref[:, LANE * h:LANE * (h + 1)], preferred_element_type=F32)
        o_ref[:, LANE * h:LANE * (h + 1)] = oh.astype(o_ref.dtype)


def sample_attention(qfull, w_uk, w_uv, cache_ckv, cache_krd, layer, ckv_b, krc_b, ob, S, Bs, T, H):
    C = w_uk.shape[0]
    P = cache_ckv.shape[2]
    KPAD = LANE
    assert T % 16 == 0 and S % T == 0 and T <= KPAD
    off = S // T
    return pl.pallas_call(
        functools.partial(_sattn_kernel, H=H, T=T, C=C, P=P, KPAD=KPAD),
        grid=(Bs,),
        in_specs=[pl.BlockSpec((T, 2 * LANE * H), lambda b: (b + off, 0)),
                  pl.BlockSpec((C, LANE * H), lambda b: (0, 0)),
                  pl.BlockSpec((C, LANE * H), lambda b: (0, 0)),
                  pl.BlockSpec((None, None, P, C), lambda b: (layer, b, 0, 0)),
                  pl.BlockSpec((None, None, P, LANE), lambda b: (layer, b, 0, 0)),
                  pl.BlockSpec((T, C), lambda b: (b + off, 0)),
                  pl.BlockSpec((T, LANE), lambda b: (b + off, 0)),
                  pl.BlockSpec(memory_space=pl.ANY)],
        out_specs=pl.BlockSpec((T, LANE * H), lambda b: (b + off, 0)),
        out_shape=jax.ShapeDtypeStruct(ob.shape, ob.dtype),
        scratch_shapes=[pltpu.VMEM((H * T, C + LANE), BF16), pltpu.VMEM((P + KPAD, C + LANE), BF16)],
        input_output_aliases={7: 0},
        compiler_params=_params("arbitrary"),
        name="sample_attention",
    )(qfull, w_uk, w_uv, cache_ckv, cache_krd, ckv_b, krc_b, ob)


def _xattn_kernel(*refs, heads, dh, scale, aliased):
    if aliased:
        q_ref, k_ref, v_ref, _, o_ref = refs
    else:
        q_ref, k_ref, v_ref, o_ref = refs
    for h in range(heads):
        sl = slice(dh * h, dh * (h + 1))
        kh = k_ref[:, sl].astype(BF16)
        vh = v_ref[:, sl].astype(BF16)
        s = lax.dot_general(q_ref[:, sl], kh, NT_DIMS, preferred_element_type=F32) * scale
        p = jnp.exp(s - jnp.max(s, axis=-1, keepdims=True))
        l = jnp.sum(p, axis=-1, keepdims=True)
        o_ref[:, sl] = (jnp.dot(p.astype(BF16), vh, preferred_element_type=F32) / l).astype(o_ref.dtype)


def cross_attention(qm, mem_k, mem_v, heads, S, Bs, T, layer=None, prev_out=None):
    R, W = qm.shape
    dh = W // heads
    prompt = prev_out is None
    bm = _pick(S, 512, 16) if prompt else T
    off = 0 if prompt else S // bm
    steps = S // bm if prompt else Bs
    if prompt:
        M = mem_k.shape[0]
        kv_spec = pl.BlockSpec((M, W), lambda i: (0, 0))
    else:
        M = mem_k.shape[2]
        kv_spec = pl.BlockSpec((None, None, M, W), lambda i: (layer, i, 0, 0))
    in_specs = [pl.BlockSpec((bm, W), lambda i: (i + off, 0)), kv_spec, kv_spec]
    args = [qm, mem_k, mem_v]
    aliases = {}
    if not prompt:
        in_specs.append(pl.BlockSpec(memory_space=pl.ANY))
        args.append(prev_out)
        aliases[3] = 0
    return pl.pallas_call(
        functools.partial(_xattn_kernel, heads=heads, dh=dh, scale=dh ** -0.5, aliased=not prompt),
        grid=(steps,),
        in_specs=in_specs,
        out_specs=pl.BlockSpec((bm, W), lambda i: (i + off, 0)),
        out_shape=jax.ShapeDtypeStruct((R, W), BF16),
        input_output_aliases=aliases,
        compiler_params=_params("arbitrary"),
        name="xattn_prompt" if prompt else "xattn_sample",
    )(*args)


def _rope_tables(S, Bs, T, P, half):
    pos = jnp.concatenate([jnp.arange(S, dtype=jnp.int32),
                           jnp.tile(P + jnp.arange(T, dtype=jnp.int32), Bs)])
    inv_freq = ROPE_BASE ** (-jnp.arange(half, dtype=F32) / half)
    ang = pos.astype(F32)[:, None] * inv_freq[None, :]
    cos, sin = jnp.cos(ang), jnp.sin(ang)
    return jnp.concatenate([cos, cos, -sin, sin], axis=-1)


def _step(x_prompt, x_sample, cache_mla_ckv, cache_mla_krope, cache_mem_k, cache_mem_v,
          state_conv_a, state_conv_ffn, mem_prompt,
          norm_mix, w_in, q_norm, kv_norm, w_uq, w_ukv, conv_a, w_out_a, w_o_mla, w_o,
          norm_xattn, mem_norm, w_mq, w_mk, w_mv, w_mo, norm_ffn, w_up, conv_ffn, w_down,
          norm_final):
    Bp, S, D = x_prompt.shape
    Bs, T, _ = x_sample.shape
    DEPTH = w_in.shape[0]
    P = cache_mla_ckv.shape[2]
    C = cache_mla_ckv.shape[3]
    ROPE = cache_mla_krope.shape[3]
    half = ROPE // 2
    A = conv_a.shape[2]
    Q = q_norm.shape[1]
    H = w_ukv.shape[2]
    NOPE = w_uq.shape[2] // H - ROPE
    V = w_ukv.shape[3] - NOPE
    MEM = mem_prompt.shape[1]
    MH, MD = cache_mem_k.shape[3], cache_mem_k.shape[4]
    F = w_down.shape[1]
    assert Bp == 1 and NOPE == LANE and V == LANE and 4 * half == LANE
    R = S + Bs * T
    scale = (NOPE + ROPE) ** -0.5

    lay = {"A": A, "Q": Q, "C": C, "b": 0, "c": A, "h": 2 * A, "q": 3 * A, "kv": 3 * A + Q,
           "kr": 3 * A + Q + C}
    used = 3 * A + Q + C + LANE
    g0 = -(-used // 1024) * 1024
    lay["ga"], lay["gb"] = g0, g0 + D
    kr0 = 3 * A + Q + C

    x = jnp.concatenate([x_prompt.reshape(S, D), x_sample.reshape(Bs * T, D)], axis=0)
    tk = _rope_tables(S, Bs, T, P, half)
    tq = tk * scale
    zeros_a = jnp.zeros((1, 2, A), F32)
    zeros_f = jnp.zeros((1, 2, 2 * F), F32)
    cache_krd = jnp.concatenate([cache_mla_krope, cache_mla_krope], axis=-1).astype(BF16)
    cmk = cache_mem_k.reshape(DEPTH, Bs, MEM, MH * MD)
    cmv = cache_mem_v.reshape(DEPTH, Bs, MEM, MH * MD)
    mem = mem_prompt.reshape(MEM, D)

    outs = {k: [] for k in ("ckv_p", "kr_p", "mk", "mv", "ca_p", "cf_p", "ckv_s", "kr_s", "ca_s", "cf_s")}
    for l in range(DEPTH):
        wl = w_in[l]
        w_in_l = jnp.concatenate(
            [wl[:, :kr0 + ROPE], wl[:, kr0 + half:kr0 + ROPE], wl[:, kr0:kr0 + half],
             jnp.zeros((D, g0 - used), F32), wl[:, kr0 + ROPE:]], axis=1).astype(BF16)
        wq = w_uq[l].reshape(Q, H, NOPE + ROPE)
        x1, x2 = wq[..., NOPE:NOPE + half], wq[..., NOPE + half:]
        w_uq_l = jnp.concatenate([wq[..., :NOPE], x1, x2, x2, x1], axis=-1).reshape(Q, H * 2 * LANE).astype(BF16)
        w_uk_l = w_ukv[l][..., :NOPE].reshape(C, H * NOPE).astype(BF16)
        w_uv_l = w_ukv[l][..., NOPE:].reshape(C, H * V).astype(BF16)

        xn = rmsnorm_rows(x, norm_mix[l], BF16)
        proj = matmul(xn, w_in_l, F32, name="matmul_in")
        po = post_in(proj, lay, zeros_a, conv_a[l], q_norm[l], kv_norm[l], tk, S, Bs, T)
        ya_in, cq, ckv, ckv_b, krc, krc_b, utail = post_in(
            proj, lay, state_conv_a[l], conv_a[l], q_norm[l], kv_norm[l], tk, S, Bs, T, prev_outs=po)
        qfull = matmul_q(cq, w_uq_l, tq, scale)
        k_pad, v = matmul_kv(ckv_b, krc_b, w_uk_l, w_uv_l, S)
        ob = prompt_attention(qfull, k_pad, v, S, R, H)
        ob = sample_attention(qfull, w_uk_l, w_uv_l, cache_mla_ckv, cache_krd, l, ckv_b, krc_b, ob, S, Bs, T, H)
        merged = matmul_merge(ya_in, w_out_a[l].astype(BF16), ob, w_o_mla[l].astype(BF16), proj,
                              lay["ga"], lay["gb"])
        x = matmul(merged, w_o[l].astype(BF16), F32, res=x, bn_target=512, name="matmul_o")

        mn = rmsnorm_rows(mem, mem_norm[l], BF16)
        mk = matmul(mn, w_mk[l].astype(BF16), F32, name="matmul_mk")
        mv = matmul(mn, w_mv[l].astype(BF16), F32, name="matmul_mv")
        xc = rmsnorm_rows(x, norm_xattn[l], BF16)
        qm = matmul(xc, w_mq[l].astype(BF16), BF16, name="matmul_mq")
        om = cross_attention(qm, mk, mv, MH, S, Bs, T)
        om = cross_attention(qm, cmk, cmv, MH, S, Bs, T, layer=l, prev_out=om)
        x = matmul(om, w_mo[l].astype(BF16), F32, res=x, bn_target=512, name="matmul_mo")

        xf = rmsnorm_rows(x, norm_ffn[l], BF16)
        up = matmul(xf, w_up[l].astype(BF16), F32, bn_target=512, name="matmul_up")
        hmid = ffn_gate(up, zeros_f, conv_ffn[l], S, Bs, T)
        hmid = ffn_gate(up, state_conv_ffn[l], conv_ffn[l], S, Bs, T, prev_out=hmid)
        x = matmul(hmid, w_down[l].astype(BF16), F32, res=x, bm_target=MM_ROW_TILE // 2, bn_target=512,
                   name="matmul_down")

        n_pt = S // _pick(S, EW_ROW_TILE, 16)
        outs["ckv_p"].append(ckv[:S].reshape(Bp, S, C))
        outs["kr_p"].append(krc[:S, :ROPE].reshape(Bp, S, ROPE))
        outs["mk"].append(mk.reshape(Bp, MEM, MH, MD))
        outs["mv"].append(mv.reshape(Bp, MEM, MH, MD))
        outs["ca_p"].append(utail[n_pt * HALO_ROWS - 2:n_pt * HALO_ROWS].reshape(Bp, 2, A))
        outs["cf_p"].append(up[S - 2:S].reshape(Bp, 2, 2 * F))
        outs["ckv_s"].append(ckv[S:].reshape(Bs, T, C))
        outs["kr_s"].append(krc[S:, :ROPE].reshape(Bs, T, ROPE))
        outs["ca_s"].append(utail[n_pt * HALO_ROWS:].reshape(Bs, HALO_ROWS, A)[:, HALO_ROWS - 2:])
        outs["cf_s"].append(up[S:].reshape(Bs, T, 2 * F)[:, T - 2:])

    y_prompt = rmsnorm_rows(x, norm_final, F32, row0=0, nrows=S).reshape(Bp, S, D)
    y_sample = rmsnorm_rows(x, norm_final, F32, row0=S, nrows=Bs * T).reshape(Bs, T, D)
    st = {k: jnp.stack(v) for k, v in outs.items()}
    return (y_prompt, y_sample, st["ckv_p"], st["kr_p"], st["mk"], st["mv"], st["ca_p"], st["cf_p"],
            st["ckv_s"], st["kr_s"], st["ca_s"], st["cf_s"])


def kernel(x_prompt, x_sample, cache_mla_ckv, cache_mla_krope, cache_mem_k, cache_mem_v, state_conv_a, state_conv_ffn, mem_prompt, norm_mix, w_in, q_norm, kv_norm, w_uq, w_ukv, conv_a, w_out_a, w_o_mla, w_o, norm_xattn, mem_norm, w_mq, w_mk, w_mv, w_mo, norm_ffn, w_up, conv_ffn, w_down, norm_final):
    return _step(x_prompt, x_sample, cache_mla_ckv, cache_mla_krope, cache_mem_k, cache_mem_v,
                 state_conv_a, state_conv_ffn, mem_prompt,
                 norm_mix, w_in, q_norm, kv_norm, w_uq, w_ukv, conv_a, w_out_a, w_o_mla, w_o,
                 norm_xattn, mem_norm, w_mq, w_mk, w_mv, w_mo, norm_ffn, w_up, conv_ffn, w_down,
                 norm_final)
```

```python
import functools

import jax
import jax.numpy as jnp
from jax import lax
from jax.experimental import pallas as pl
from jax.experimental.pallas import tpu as pltpu

F32 = jnp.float32
BF16 = jnp.bfloat16

EPS = 1e-6
CHUNK = 64
ROPE_BASE = 10000.0
NEG_BIG = -1e30

VMEM_LIMIT_BYTES = 60 * 1024 * 1024
LANE = 128
SUBLANE = 8
HALO_ROWS = 8
MM_ROW_TILE = 1040
EW_ROW_TILE = 256
FA_BLOCK_Q = 2048
FA_SUB_ROWS = 1024
FA_BLOCK_K = 512
FA_BLOCK_K_MAIN = 2048
LOG2E = 1.4426950408889634
NT_DIMS = (((1,), (1,)), ((), ()))


def _pick(n, target, mult):
    best = None
    for d in range(mult, min(n, target) + 1, mult):
        if n % d == 0:
            best = d
    return best or n


def _params(*sem):
    return pltpu.CompilerParams(dimension_semantics=sem, vmem_limit_bytes=VMEM_LIMIT_BYTES)


def _rms(x, g):
    return x * lax.rsqrt(jnp.mean(x * x, axis=-1, keepdims=True) + EPS) * g


def _rms_kernel(x_ref, g_ref, o_ref):
    o_ref[...] = _rms(x_ref[...].astype(F32), g_ref[...]).astype(o_ref.dtype)


def rmsnorm_rows(x, g, out_dtype, row0=0, nrows=None, bm_target=512):
    R, C = x.shape
    nrows = R - row0 if nrows is None else nrows
    bm = _pick(nrows, bm_target, 16)
    assert row0 % bm == 0
    off = row0 // bm
    return pl.pallas_call(
        _rms_kernel,
        grid=(nrows // bm,),
        in_specs=[pl.BlockSpec((bm, C), lambda i: (i + off, 0)),
                  pl.BlockSpec((1, C), lambda i: (0, 0))],
        out_specs=pl.BlockSpec((bm, C), lambda i: (i, 0)),
        out_shape=jax.ShapeDtypeStruct((nrows, C), out_dtype),
        compiler_params=_params("parallel"),
        name="rmsnorm",
    )(x, g.reshape(1, C).astype(F32))


def _mm_kernel(a_ref, w_ref, o_ref):
    o_ref[...] = jnp.dot(a_ref[...], w_ref[...], preferred_element_type=F32).astype(o_ref.dtype)


def _mm_res_kernel(a_ref, w_ref, r_ref, o_ref):
    acc = jnp.dot(a_ref[...], w_ref[...], preferred_element_type=F32)
    o_ref[...] = (r_ref[...] + acc).astype(o_ref.dtype)


def _wspec(w, layer, bn):
    K = w.shape[-2]
    if w.ndim == 3:
        return pl.BlockSpec((None, K, bn), lambda i, j: (layer, 0, j))
    return pl.BlockSpec((K, bn), lambda i, j: (0, j))


def matmul(a, w, out_dtype, res=None, layer=None, bm_target=MM_ROW_TILE, bn_target=1024, name="matmul"):
    M, K = a.shape
    N = w.shape[-1]
    bm = _pick(M, bm_target, 16)
    bn = _pick(N, bn_target, LANE)
    in_specs = [pl.BlockSpec((bm, K), lambda i, j: (i, 0)), _wspec(w, layer, bn)]
    args = [a, w]
    body = _mm_kernel
    if res is not None:
        in_specs.append(pl.BlockSpec((bm, bn), lambda i, j: (i, j)))
        args.append(res)
        body = _mm_res_kernel
    return pl.pallas_call(
        body,
        grid=(M // bm, N // bn),
        in_specs=in_specs,
        out_specs=pl.BlockSpec((bm, bn), lambda i, j: (i, j)),
        out_shape=jax.ShapeDtypeStruct((M, N), out_dtype),
        compiler_params=_params("parallel", "parallel"),
        name=name,
    )(*args)


def _mm_q_kernel(a_ref, w_ref, t_ref, o_ref, *, heads, scale):
    acc = jnp.dot(a_ref[...], w_ref[...], preferred_element_type=F32)
    t = t_ref[...]
    for h in range(heads):
        c0 = 2 * LANE * h
        o_ref[:, c0:c0 + LANE] = (acc[:, c0:c0 + LANE] * scale).astype(o_ref.dtype)
        o_ref[:, c0 + LANE:c0 + 2 * LANE] = (acc[:, c0 + LANE:c0 + 2 * LANE] * t).astype(o_ref.dtype)


def matmul_q(a, w, layer, tq, scale, heads_per_tile=4):
    M, K = a.shape
    N = w.shape[-1]
    bm = _pick(M, MM_ROW_TILE, 16)
    bn = 2 * LANE * heads_per_tile
    assert N % bn == 0
    return pl.pallas_call(
        functools.partial(_mm_q_kernel, heads=heads_per_tile, scale=scale),
        grid=(M // bm, N // bn),
        in_specs=[pl.BlockSpec((bm, K), lambda i, j: (i, 0)),
                  _wspec(w, layer, bn),
                  pl.BlockSpec((bm, LANE), lambda i, j: (i, 0))],
        out_specs=pl.BlockSpec((bm, bn), lambda i, j: (i, j)),
        out_shape=jax.ShapeDtypeStruct((M, N), BF16),
        compiler_params=_params("parallel", "parallel"),
        name="matmul_q",
    )(a, w, tq)


def _mm_kv_kernel(a_ref, wk_ref, wv_ref, kr_ref, k_ref, v_ref, *, heads):
    a = a_ref[...]
    kn = jnp.dot(a, wk_ref[...], preferred_element_type=F32)
    vn = jnp.dot(a, wv_ref[...], preferred_element_type=F32)
    kr = kr_ref[...]
    ones = jnp.ones(kr.shape, v_ref.dtype)
    for h in range(heads):
        lo, mid, hi = 2 * LANE * h, 2 * LANE * h + LANE, 2 * LANE * (h + 1)
        k_ref[:, lo:mid] = kn[:, LANE * h:LANE * (h + 1)].astype(k_ref.dtype)
        k_ref[:, mid:hi] = kr
        v_ref[:, lo:mid] = vn[:, LANE * h:LANE * (h + 1)].astype(v_ref.dtype)
        v_ref[:, mid:hi] = ones


def matmul_kv(ckv_b, krc_b, w_uk, w_uv, layer, S, heads_per_tile=4):
    C = ckv_b.shape[1]
    N = w_uk.shape[-1]
    bm = _pick(S, 1024, 16)
    bn = LANE * heads_per_tile
    assert N % bn == 0
    return pl.pallas_call(
        functools.partial(_mm_kv_kernel, heads=heads_per_tile),
        grid=(S // bm, N // bn),
        in_specs=[pl.BlockSpec((bm, C), lambda i, j: (i, 0)),
                  _wspec(w_uk, layer, bn),
                  _wspec(w_uv, layer, bn),
                  pl.BlockSpec((bm, LANE), lambda i, j: (i, 0))],
        out_specs=[pl.BlockSpec((bm, 2 * bn), lambda i, j: (i, j)),
                   pl.BlockSpec((bm, 2 * bn), lambda i, j: (i, j))],
        out_shape=[jax.ShapeDtypeStruct((S, 2 * N), BF16),
                   jax.ShapeDtypeStruct((S, 2 * N), BF16)],
        compiler_params=_params("parallel", "parallel"),
        name="matmul_kv",
    )(ckv_b, w_uk, w_uv, krc_b)


def _sigmoid(x):
    return 1.0 / (1.0 + jnp.exp(-x))


def _mm_merge_kernel(a1_ref, w1_ref, a2_ref, w2_ref, ga_ref, gb_ref, o_ref):
    ya = jnp.dot(a1_ref[...], w1_ref[...], preferred_element_type=F32)
    yb = jnp.dot(a2_ref[...], w2_ref[...], preferred_element_type=F32)
    o_ref[...] = (_sigmoid(ga_ref[...]) * ya + _sigmoid(gb_ref[...]) * yb).astype(o_ref.dtype)


def matmul_merge(ya_in, w_out_a, ob, w_o_mla, layer, proj, ga_col0, gb_col0):
    M, K1 = ya_in.shape
    K2 = ob.shape[1]
    N = w_out_a.shape[-1]
    bm = _pick(M, MM_ROW_TILE, 16)
    bn = _pick(N, 512, LANE)
    assert ga_col0 % bn == 0 and gb_col0 % bn == 0
    ja, jb = ga_col0 // bn, gb_col0 // bn
    return pl.pallas_call(
        _mm_merge_kernel,
        grid=(M // bm, N // bn),
        in_specs=[pl.BlockSpec((bm, K1), lambda i, j: (i, 0)),
                  _wspec(w_out_a, layer, bn),
                  pl.BlockSpec((bm, K2), lambda i, j: (i, 0)),
                  _wspec(w_o_mla, layer, bn),
                  pl.BlockSpec((bm, bn), lambda i, j: (i, j + ja)),
                  pl.BlockSpec((bm, bn), lambda i, j: (i, j + jb))],
        out_specs=pl.BlockSpec((bm, bn), lambda i, j: (i, j)),
        out_shape=jax.ShapeDtypeStruct((M, N), BF16),
        compiler_params=_params("parallel", "parallel"),
        name="matmul_merge",
    )(ya_in, w_out_a, ob, w_o_mla, proj, proj)


def _conv3(u, h0, h1, w):
    row = lax.broadcasted_iota(jnp.int32, (u.shape[0], 1), 0)
    p1 = jnp.where(row == 0, h1, pltpu.roll(u, 1, 0))
    p2 = jnp.where(row == 0, h0, jnp.where(row == 1, h1, pltpu.roll(u, 2, 0)))
    return w[0:1] * p2 + w[1:2] * p1 + w[2:3] * u


def _halo(prev_u, st_ref, first):
    st = st_ref[...]
    if prev_u is None:
        return st[0:1], st[1:2]
    return (jnp.where(first, st[0:1], prev_u[HALO_ROWS - 2:HALO_ROWS - 1]),
            jnp.where(first, st[1:2], prev_u[HALO_ROWS - 1:HALO_ROWS]))


def _postin_kernel(*refs, has_prev):
    if has_prev:
        (b_ref, c_ref, h_ref, cp_ref, hp_ref, ql_ref, kv_ref, kr_ref, st_ref, taps_ref, qn_ref, kvn_ref,
         t_ref, ya_ref, cq_ref, ckv_ref, ckvb_ref, krc_ref, krcb_ref, ut_ref) = refs
        prev_u = cp_ref[...] * hp_ref[...]
    else:
        (b_ref, c_ref, h_ref, ql_ref, kv_ref, kr_ref, st_ref, taps_ref, qn_ref, kvn_ref,
         t_ref, ya_ref, cq_ref, ckv_ref, ckvb_ref, krc_ref, krcb_ref, ut_ref) = refs
        prev_u = None
    u = c_ref[...] * h_ref[...]
    h0, h1 = _halo(prev_u, st_ref, pl.program_id(0) == 0)
    ya_ref[...] = (b_ref[...] * _conv3(u, h0, h1, taps_ref[...])).astype(ya_ref.dtype)
    ut_ref[...] = u[u.shape[0] - HALO_ROWS:, :]
    cq_ref[...] = _rms(ql_ref[...], qn_ref[...]).astype(cq_ref.dtype)
    ckv = _rms(kv_ref[...], kvn_ref[...])
    ckv_ref[...] = ckv
    ckvb_ref[...] = ckv.astype(ckvb_ref.dtype)
    t = kr_ref[...] * t_ref[...]
    krc = t + pltpu.roll(t, LANE // 2, 1)
    krc_ref[...] = krc
    krcb_ref[...] = krc.astype(krcb_ref.dtype)


def post_in(proj, lay, state, taps, q_norm, kv_norm, tk, S, Bs, T, prev_outs=None):
    R = proj.shape[0]
    A, Q, C = lay["A"], lay["Q"], lay["C"]
    prompt = prev_outs is None
    bm = _pick(S, EW_ROW_TILE, 16) if prompt else T
    assert bm % 16 == 0 and S % bm == 0
    off = 0 if prompt else S // bm
    steps = S // bm if prompt else Bs
    rb = bm // HALO_ROWS

    def col(width, col0):
        assert col0 % width == 0
        return pl.BlockSpec((bm, width), lambda i, c=col0 // width: (i + off, c))

    in_specs = [col(A, lay["b"]), col(A, lay["c"]), col(A, lay["h"])]
    args = [proj, proj, proj]
    if prompt:
        for name in ("c", "h"):
            in_specs.append(pl.BlockSpec((HALO_ROWS, A),
                                         lambda i, c=lay[name] // A: (jnp.maximum(i * rb - 1, 0), c)))
            args.append(proj)
    in_specs += [col(Q, lay["q"]), col(C, lay["kv"]), col(LANE, lay["kr"]),
                 pl.BlockSpec((None, 2, A), (lambda i: (0, 0, 0)) if prompt else (lambda i: (i, 0, 0))),
                 pl.BlockSpec((3, A), lambda i: (0, 0)),
                 pl.BlockSpec((1, Q), lambda i: (0, 0)),
                 pl.BlockSpec((1, C), lambda i: (0, 0)),
                 pl.BlockSpec((bm, LANE), lambda i: (i + off, 0))]
    args += [proj, proj, proj, state, taps, q_norm.reshape(1, Q), kv_norm.reshape(1, C), tk]
    out_widths = [(A, BF16), (Q, BF16), (C, F32), (C, BF16), (LANE, F32), (LANE, BF16)]
    out_specs = [pl.BlockSpec((bm, w), lambda i: (i + off, 0)) for w, _ in out_widths]
    out_shape = [jax.ShapeDtypeStruct((R, w), dt) for w, dt in out_widths]
    n_pt = S // _pick(S, EW_ROW_TILE, 16)
    toff = 0 if prompt else n_pt
    out_specs.append(pl.BlockSpec((HALO_ROWS, A), lambda i: (i + toff, 0)))
    out_shape.append(jax.ShapeDtypeStruct(((n_pt + Bs) * HALO_ROWS, A), F32))
    aliases = {}
    if not prompt:
        n_in = len(args)
        for k, arr in enumerate(prev_outs):
            in_specs.append(pl.BlockSpec(memory_space=pl.ANY))
            args.append(arr)
            aliases[n_in + k] = k
    return pl.pallas_call(
        functools.partial(_postin_kernel_aliased if not prompt else _postin_kernel, has_prev=prompt),
        grid=(steps,),
        in_specs=in_specs,
        out_specs=out_specs,
        out_shape=out_shape,
        input_output_aliases=aliases,
        compiler_params=_params("arbitrary"),
        name="post_in_prompt" if prompt else "post_in_sample",
    )(*args)


def _postin_kernel_aliased(*refs, has_prev):
    n_out = 7
    n_in = len(refs) - 2 * n_out
    _postin_kernel(*refs[:n_in], *refs[n_in + n_out:], has_prev=has_prev)


def _ffn_kernel(*refs, has_prev, aliased):
    if aliased:
        refs = refs[:-2] + refs[-1:]
    if has_prev:
        a_ref, g_ref, ap_ref, gp_ref, sa_ref, sg_ref, wa_ref, wg_ref, o_ref = refs
        pa, pg = ap_ref[...], gp_ref[...]
    else:
        a_ref, g_ref, sa_ref, sg_ref, wa_ref, wg_ref, o_ref = refs
        pa = pg = None
    first = pl.program_id(0) == 0
    a0, a1 = _halo(pa, sa_ref, first)
    g0, g1 = _halo(pg, sg_ref, first)
    a = _conv3(a_ref[...], a0, a1, wa_ref[...])
    g = _conv3(g_ref[...], g0, g1, wg_ref[...])
    o_ref[...] = (g * _sigmoid(g) * a).astype(o_ref.dtype)


def ffn_gate(up, state, taps, S, Bs, T, prev_out=None):
    R, F2 = up.shape
    F = F2 // 2
    prompt = prev_out is None
    bm = _pick(S, EW_ROW_TILE, 16) if prompt else T
    bc = _pick(F, 5504, LANE)
    nc = F // bc
    off = 0 if prompt else S // bm
    steps = S // bm if prompt else Bs
    rb = bm // HALO_ROWS
    in_specs = [pl.BlockSpec((bm, bc), lambda i, j: (i + off, j)),
                pl.BlockSpec((bm, bc), lambda i, j: (i + off, j + nc))]
    args = [up, up]
    if prompt:
        in_specs += [pl.BlockSpec((HALO_ROWS, bc), lambda i, j: (jnp.maximum(i * rb - 1, 0), j)),
                     pl.BlockSpec((HALO_ROWS, bc), lambda i, j: (jnp.maximum(i * rb - 1, 0), j + nc))]
        args += [up, up]
    bsel = (lambda i: 0) if prompt else (lambda i: i)
    in_specs += [pl.BlockSpec((None, 2, bc), lambda i, j: (bsel(i), 0, j)),
                 pl.BlockSpec((None, 2, bc), lambda i, j: (bsel(i), 0, j + nc)),
                 pl.BlockSpec((3, bc), lambda i, j: (0, j)),
                 pl.BlockSpec((3, bc), lambda i, j: (0, j + nc))]
    args += [state, state, taps, taps]
    aliases = {}
    if not prompt:
        in_specs.append(pl.BlockSpec(memory_space=pl.ANY))
        args.append(prev_out)
        aliases[len(args) - 1] = 0
    return pl.pallas_call(
        functools.partial(_ffn_kernel, has_prev=prompt, aliased=not prompt),
        grid=(steps, nc),
        in_specs=in_specs,
        out_specs=pl.BlockSpec((bm, bc), lambda i, j: (i + off, j)),
        out_shape=jax.ShapeDtypeStruct((R, F), BF16),
        input_output_aliases=aliases,
        compiler_params=_params("arbitrary", "arbitrary"),
        name="ffn_gate_prompt" if prompt else "ffn_gate_sample",
    )(*args)


def _fa_kernel(q_ref, k_ref, v_ref, o_ref, m_sc, acc_sc, *, bq, bk, bk_main, sub):
    i = pl.program_id(1)
    m_sc[...] = jnp.full(m_sc.shape, NEG_BIG, F32)
    acc_sc[...] = jnp.zeros(acc_sc.shape, F32)
    chains = bq // sub

    def chain_block(r, k0, bk, local_k0):
        lane_tiles = bk // LANE
        rows = slice(sub * r, sub * (r + 1))
        kj = k_ref[pl.ds(k0, bk), :]
        vj = v_ref[pl.ds(k0, bk), :]
        s = lax.dot_general(q_ref[rows, :], kj, NT_DIMS, preferred_element_type=F32)
        if local_k0 is not None:
            kchunk = (lax.broadcasted_iota(jnp.int32, (sub, bk), 1) + local_k0) // CHUNK
            qchunk = (lax.broadcasted_iota(jnp.int32, (sub, bk), 0) + sub * r) // CHUNK
            s = jnp.where(kchunk <= qchunk, s, NEG_BIG)
        cols = [s[:, LANE * c:LANE * (c + 1)] for c in range(lane_tiles)]
        m_cur = cols[0]
        for c in cols[1:]:
            m_cur = jnp.maximum(m_cur, c)
        m_prev = m_sc[rows, :]
        m_new = jnp.maximum(m_prev, jnp.max(m_cur, axis=-1, keepdims=True))
        alpha = jnp.exp2(m_prev - m_new)
        p = jnp.concatenate([jnp.exp2(c - m_new) for c in cols], axis=1).astype(vj.dtype)
        pv = jnp.dot(p, vj, preferred_element_type=F32)
        acc_sc[rows, :] = jnp.concatenate([alpha, alpha], axis=1) * acc_sc[rows, :] + pv
        m_sc[rows, :] = m_new

    def body(j, carry):
        k0 = pl.multiple_of(j * bk_main, bk_main)
        for b in range(bk_main // bk):
            for r in range(chains):
                chain_block(r, k0 + b * bk, bk, None)
        return carry

    lax.fori_loop(0, i * (bq // bk_main), body, 0)
    t0 = pl.multiple_of(i * bq, bq)
    for b in range(bq // bk):
        for r in range(chains):
            if b * bk < (r + 1) * sub:
                chain_block(r, t0 + b * bk, bk, None if (b + 1) * bk <= r * sub else b * bk)
    acc = acc_sc[...]
    o_ref[...] = (acc[:, :LANE] / acc[:, LANE:]).astype(o_ref.dtype)


def prompt_attention(q_pad, k_pad, v_ext, S, R, H):
    bq = _pick(S, FA_BLOCK_Q, CHUNK)
    sub = _pick(bq, FA_SUB_ROWS, CHUNK)
    bk = _pick(sub, FA_BLOCK_K, LANE)
    bk_main = _pick(bq, FA_BLOCK_K_MAIN, bk)
    assert bq % bk_main == 0 and sub % bk == 0 and S % bq == 0
    return pl.pallas_call(
        functools.partial(_fa_kernel, bq=bq, bk=bk, bk_main=bk_main, sub=sub),
        grid=(H, S // bq),
        in_specs=[pl.BlockSpec((bq, 2 * LANE), lambda h, i: (i, h)),
                  pl.BlockSpec((S, 2 * LANE), lambda h, i: (0, h)),
                  pl.BlockSpec((S, 2 * LANE), lambda h, i: (0, h))],
        out_specs=pl.BlockSpec((bq, LANE), lambda h, i: (i, h)),
        out_shape=jax.ShapeDtypeStruct((R, H * LANE), BF16),
        scratch_shapes=[pltpu.VMEM((bq, LANE), F32), pltpu.VMEM((bq, 2 * LANE), F32)],
        compiler_params=_params("parallel", "arbitrary"),
        name="prompt_attention",
    )(q_pad, k_pad, v_ext)


def _sattn_kernel(q_ref, wuk_ref, wuv_ref, cckv_ref, ckr_ref, nckv_ref, nkr_ref, ob_ref, o_ref,
                  ql_sc, ka_sc, *, H, T, C, P, KPAD):
    del ob_ref
    for h in range(H):
        qn = q_ref[:, 2 * LANE * h:2 * LANE * h + LANE]
        qlat = lax.dot_general(qn, wuk_ref[:, LANE * h:LANE * (h + 1)], NT_DIMS, preferred_element_type=F32)
        ql_sc[T * h:T * (h + 1), 0:C] = qlat.astype(ql_sc.dtype)
        ql_sc[T * h:T * (h + 1), C:C + LANE] = q_ref[:, 2 * LANE * h + LANE:2 * LANE * (h + 1)]
    ka_sc[0:P, 0:C] = cckv_ref[...].astype(ka_sc.dtype)
    ka_sc[0:P, C:C + LANE] = ckr_ref[...]
    ka_sc[P:P + T, 0:C] = nckv_ref[...]
    ka_sc[P:P + T, C:C + LANE] = nkr_ref[...]
    ka_sc[P + T:P + KPAD, :] = jnp.zeros((KPAD - T, C + LANE), ka_sc.dtype)
    ka = ka_sc[...]
    s = lax.dot_general(ql_sc[...], ka, NT_DIMS, preferred_element_type=F32)
    kpos = lax.broadcasted_iota(jnp.int32, s.shape, 1)
    qpos = P + lax.broadcasted_iota(jnp.int32, s.shape, 0) % T
    s = jnp.where((kpos < P + T) & (kpos // CHUNK <= qpos // CHUNK), s, NEG_BIG)
    m = jnp.max(s, axis=-1, keepdims=True)
    p = jnp.exp2(s - m)
    l = jnp.sum(p, axis=-1, keepdims=True)
    olat = (jnp.dot(p.astype(ka.dtype), ka[:, 0:C], preferred_element_type=F32) / l).astype(wuv_ref.dtype)
    for h in range(H):
        oh = jnp.dot(olat[T * h:T * (h + 1), :], wuv_ref[:, LANE * h:LANE * (h + 1)], preferred_element_type=F32)
        o_ref[:, LANE * h:LANE * (h + 1)] = oh.astype(o_ref.dtype)


def sample_attention(qfull, w_uk, w_uv, cache_ckv, cache_krd, layer, ckv_b, krc_b, ob, S, Bs, T, H):
    C = w_uk.shape[-2]
    P = cache_ckv.shape[2]
    KPAD = LANE
    assert T % 16 == 0 and S % T == 0 and T <= KPAD and w_uk.ndim == 3
    off = S // T
    return pl.pallas_call(
        functools.partial(_sattn_kernel, H=H, T=T, C=C, P=P, KPAD=KPAD),
        grid=(Bs,),
        in_specs=[pl.BlockSpec((T, 2 * LANE * H), lambda b: (b + off, 0)),
                  pl.BlockSpec((None, C, LANE * H), lambda b: (layer, 0, 0)),
                  pl.BlockSpec((None, C, LANE * H), lambda b: (layer, 0, 0)),
                  pl.BlockSpec((None, None, P, C), lambda b: (layer, b, 0, 0)),
                  pl.BlockSpec((None, None, P, LANE), lambda b: (layer, b, 0, 0)),
                  pl.BlockSpec((T, C), lambda b: (b + off, 0)),
                  pl.BlockSpec((T, LANE), lambda b: (b + off, 0)),
                  pl.BlockSpec(memory_space=pl.ANY)],
        out_specs=pl.BlockSpec((T, LANE * H), lambda b: (b + off, 0)),
        out_shape=jax.ShapeDtypeStruct(ob.shape, ob.dtype),
        scratch_shapes=[pltpu.VMEM((H * T, C + LANE), BF16), pltpu.VMEM((P + KPAD, C + LANE), BF16)],
        input_output_aliases={7: 0},
        compiler_params=_params("arbitrary"),
        name="sample_attention",
    )(qfull, w_uk, w_uv, cache_ckv, cache_krd, ckv_b, krc_b, ob)


def _xattn_kernel(*refs, heads, dh, scale, aliased):
    if aliased:
        q_ref, k_ref, v_ref, _, o_ref = refs
    else:
        q_ref, k_ref, v_ref, o_ref = refs
    for h in range(heads):
        sl = slice(dh * h, dh * (h + 1))
        kh = k_ref[:, sl].astype(BF16)
        vh = v_ref[:, sl].astype(BF16)
        s = lax.dot_general(q_ref[:, sl], kh, NT_DIMS, preferred_element_type=F32) * scale
        p = jnp.exp(s - jnp.max(s, axis=-1, keepdims=True))
        l = jnp.sum(p, axis=-1, keepdims=True)
        o_ref[:, sl] = (jnp.dot(p.astype(BF16), vh, preferred_element_type=F32) / l).astype(o_ref.dtype)


def cross_attention(qm, mem_k, mem_v, heads, S, Bs, T, layer=None, prev_out=None):
    R, W = qm.shape
    dh = W // heads
    prompt = prev_out is None
    bm = _pick(S, 512, 16) if prompt else T
    off = 0 if prompt else S // bm
    steps = S // bm if prompt else Bs
    if prompt:
        M = mem_k.shape[0]
        kv_spec = pl.BlockSpec((M, W), lambda i: (0, 0))
    else:
        M = mem_k.shape[2]
        kv_spec = pl.BlockSpec((None, None, M, W), lambda i: (layer, i, 0, 0))
    in_specs = [pl.BlockSpec((bm, W), lambda i: (i + off, 0)), kv_spec, kv_spec]
    args = [qm, mem_k, mem_v]
    aliases = {}
    if not prompt:
        in_specs.append(pl.BlockSpec(memory_space=pl.ANY))
        args.append(prev_out)
        aliases[3] = 0
    return pl.pallas_call(
        functools.partial(_xattn_kernel, heads=heads, dh=dh, scale=dh ** -0.5, aliased=not prompt),
        grid=(steps,),
        in_specs=in_specs,
        out_specs=pl.BlockSpec((bm, W), lambda i: (i + off, 0)),
        out_shape=jax.ShapeDtypeStruct((R, W), BF16),
        input_output_aliases=aliases,
        compiler_params=_params("arbitrary"),
        name="xattn_prompt" if prompt else "xattn_sample",
    )(*args)


def _rope_tables(S, Bs, T, P, half):
    pos = jnp.concatenate([jnp.arange(S, dtype=jnp.int32),
                           jnp.tile(P + jnp.arange(T, dtype=jnp.int32), Bs)])
    inv_freq = ROPE_BASE ** (-jnp.arange(half, dtype=F32) / half)
    ang = pos.astype(F32)[:, None] * inv_freq[None, :]
    cos, sin = jnp.cos(ang), jnp.sin(ang)
    return jnp.concatenate([cos, cos, -sin, sin], axis=-1)


def _step(x_prompt, x_sample, cache_mla_ckv, cache_mla_krope, cache_mem_k, cache_mem_v,
          state_conv_a, state_conv_ffn, mem_prompt,
          norm_mix, w_in, q_norm, kv_norm, w_uq, w_ukv, conv_a, w_out_a, w_o_mla, w_o,
          norm_xattn, mem_norm, w_mq, w_mk, w_mv, w_mo, norm_ffn, w_up, conv_ffn, w_down,
          norm_final):
    Bp, S, D = x_prompt.shape
    Bs, T, _ = x_sample.shape
    DEPTH = w_in.shape[0]
    P = cache_mla_ckv.shape[2]
    C = cache_mla_ckv.shape[3]
    ROPE = cache_mla_krope.shape[3]
    half = ROPE // 2
    A = conv_a.shape[2]
    Q = q_norm.shape[1]
    H = w_ukv.shape[2]
    NOPE = w_uq.shape[2] // H - ROPE
    V = w_ukv.shape[3] - NOPE
    MEM = mem_prompt.shape[1]
    MH, MD = cache_mem_k.shape[3], cache_mem_k.shape[4]
    F = w_down.shape[1]
    assert Bp == 1 and NOPE == LANE and V == LANE and 4 * half == LANE
    R = S + Bs * T
    scale = (NOPE + ROPE) ** -0.5

    lay = {"A": A, "Q": Q, "C": C, "b": 0, "c": A, "h": 2 * A, "q": 3 * A, "kv": 3 * A + Q,
           "kr": 3 * A + Q + C}
    used = 3 * A + Q + C + LANE
    g0 = -(-used // 1024) * 1024
    lay["ga"], lay["gb"] = g0, g0 + D
    kr0 = 3 * A + Q + C

    x = jnp.concatenate([x_prompt.reshape(S, D), x_sample.reshape(Bs * T, D)], axis=0)
    tk = _rope_tables(S, Bs, T, P, half)
    qscale = scale * LOG2E
    tq = tk * qscale
    zeros_a = jnp.zeros((1, 2, A), F32)
    zeros_f = jnp.zeros((1, 2, 2 * F), F32)
    cache_krd = jnp.concatenate([cache_mla_krope, cache_mla_krope], axis=-1).astype(BF16)
    cmk = cache_mem_k.reshape(DEPTH, Bs, MEM, MH * MD)
    cmv = cache_mem_v.reshape(DEPTH, Bs, MEM, MH * MD)
    mem = mem_prompt.reshape(MEM, D)

    bf = lambda w: w.astype(BF16)
    w_in_b = jnp.concatenate(
        [bf(w_in[:, :, :kr0 + ROPE]), bf(w_in[:, :, kr0 + half:kr0 + ROPE]), bf(w_in[:, :, kr0:kr0 + half]),
         jnp.zeros((DEPTH, D, g0 - used), BF16), bf(w_in[:, :, kr0 + ROPE:])], axis=2)
    wq = w_uq.reshape(DEPTH, Q, H, NOPE + ROPE)
    x1, x2 = bf(wq[..., NOPE:NOPE + half]), bf(wq[..., NOPE + half:])
    w_uq_b = jnp.concatenate([bf(wq[..., :NOPE]), x1, x2, x2, x1], axis=-1).reshape(DEPTH, Q, H * 2 * LANE)
    w_uk_b = bf(w_ukv[..., :NOPE]).reshape(DEPTH, C, H * NOPE)
    w_uv_b = bf(w_ukv[..., NOPE:]).reshape(DEPTH, C, H * V)
    w_out_a_b, w_o_mla_b, w_o_b = bf(w_out_a), bf(w_o_mla), bf(w_o)
    w_mq_b, w_mk_b, w_mv_b, w_mo_b = bf(w_mq), bf(w_mk), bf(w_mv), bf(w_mo)
    w_up_b, w_down_b = bf(w_up), bf(w_down)

    outs = {k: [] for k in ("ckv_p", "kr_p", "mk", "mv", "ca_p", "cf_p", "ckv_s", "kr_s", "ca_s", "cf_s")}
    for l in range(DEPTH):
        xn = rmsnorm_rows(x, norm_mix[l], BF16)
        proj = matmul(xn, w_in_b, F32, layer=l, name="matmul_in")
        po = post_in(proj, lay, zeros_a, conv_a[l], q_norm[l], kv_norm[l], tk, S, Bs, T)
        ya_in, cq, ckv, ckv_b, krc, krc_b, utail = post_in(
            proj, lay, state_conv_a[l], conv_a[l], q_norm[l], kv_norm[l], tk, S, Bs, T, prev_outs=po)
        qfull = matmul_q(cq, w_uq_b, l, tq, qscale)
        k_pad, v_ext = matmul_kv(ckv_b, krc_b, w_uk_b, w_uv_b, l, S)
        ob = prompt_attention(qfull, k_pad, v_ext, S, R, H)
        ob = sample_attention(qfull, w_uk_b, w_uv_b, cache_mla_ckv, cache_krd, l, ckv_b, krc_b, ob, S, Bs, T, H)
        merged = matmul_merge(ya_in, w_out_a_b, ob, w_o_mla_b, l, proj, lay["ga"], lay["gb"])
        x = matmul(merged, w_o_b, F32, res=x, layer=l, bn_target=512, name="matmul_o")

        mn = rmsnorm_rows(mem, mem_norm[l], BF16)
        mk = matmul(mn, w_mk_b, F32, layer=l, name="matmul_mk")
        mv = matmul(mn, w_mv_b, F32, layer=l, name="matmul_mv")
        xc = rmsnorm_rows(x, norm_xattn[l], BF16)
        qm = matmul(xc, w_mq_b, BF16, layer=l, name="matmul_mq")
        om = cross_attention(qm, mk, mv, MH, S, Bs, T)
        om = cross_attention(qm, cmk, cmv, MH, S, Bs, T, layer=l, prev_out=om)
        x = matmul(om, w_mo_b, F32, res=x, layer=l, bn_target=512, name="matmul_mo")

        xf = rmsnorm_rows(x, norm_ffn[l], BF16)
        up = matmul(xf, w_up_b, F32, layer=l, bn_target=512, name="matmul_up")
        hmid = ffn_gate(up, zeros_f, conv_ffn[l], S, Bs, T)
        hmid = ffn_gate(up, state_conv_ffn[l], conv_ffn[l], S, Bs, T, prev_out=hmid)
        x = matmul(hmid, w_down_b, F32, res=x, layer=l, bm_target=MM_ROW_TILE // 2, bn_target=512,
                   name="matmul_down")

        n_pt = S // _pick(S, EW_ROW_TILE, 16)
        outs["ckv_p"].append(ckv[:S].reshape(Bp, S, C))
        outs["kr_p"].append(krc[:S, :ROPE].reshape(Bp, S, ROPE))
        outs["mk"].append(mk.reshape(Bp, MEM, MH, MD))
        outs["mv"].append(mv.reshape(Bp, MEM, MH, MD))
        outs["ca_p"].append(utail[n_pt * HALO_ROWS - 2:n_pt * HALO_ROWS].reshape(Bp, 2, A))
        outs["cf_p"].append(up[S - 2:S].reshape(Bp, 2, 2 * F))
        outs["ckv_s"].append(ckv[S:].reshape(Bs, T, C))
        outs["kr_s"].append(krc[S:, :ROPE].reshape(Bs, T, ROPE))
        outs["ca_s"].append(utail[n_pt * HALO_ROWS:].reshape(Bs, HALO_ROWS, A)[:, HALO_ROWS - 2:])
        outs["cf_s"].append(up[S:].reshape(Bs, T, 2 * F)[:, T - 2:])

    y_prompt = rmsnorm_rows(x, norm_final, F32, row0=0, nrows=S).reshape(Bp, S, D)
    y_sample = rmsnorm_rows(x, norm_final, F32, row0=S, nrows=Bs * T).reshape(Bs, T, D)
    st = {k: jnp.stack(v) for k, v in outs.items()}
    return (y_prompt, y_sample, st["ckv_p"], st["kr_p"], st["mk"], st["mv"], st["ca_p"], st["cf_p"],
            st["ckv_s"], st["kr_s"], st["ca_s"], st["cf_s"])


def kernel(x_prompt, x_sample, cache_mla_ckv, cache_mla_krope, cache_mem_k, cache_mem_v, state_conv_a, state_conv_ffn, mem_prompt, norm_mix, w_in, q_norm, kv_norm, w_uq, w_ukv, conv_a, w_out_a, w_o_mla, w_o, norm_xattn, mem_norm, w_mq, w_mk, w_mv, w_mo, norm_ffn, w_up, conv_ffn, w_down, norm_final):
    return _step(x_prompt, x_sample, cache_mla_ckv, cache_mla_krope, cache_mem_k, cache_mem_v,
                 state_conv_a, state_conv_ffn, mem_prompt,
                 norm_mix, w_in, q_norm, kv_norm, w_uq, w_ukv, conv_a, w_out_a, w_o_mla, w_o,
                 norm_xattn, mem_norm, w_mq, w_mk, w_mv, w_mo, norm_ffn, w_up, conv_ffn, w_down,
                 norm_final)
```

```python
import functools

import jax
import jax.numpy as jnp
from jax import lax
from jax.experimental import pallas as pl
from jax.experimental.pallas import tpu as pltpu

F32 = jnp.float32
BF16 = jnp.bfloat16

EPS = 1e-6
CHUNK = 64
ROPE_BASE = 10000.0
NEG_BIG = -1e30

VMEM_LIMIT_BYTES = 60 * 1024 * 1024
LANE = 128
SUBLANE = 8
HALO_ROWS = 8
MM_ROW_TILE = 1040
EW_ROW_TILE = 256
FA_BLOCK_Q = 2048
FA_SUB_ROWS = 1024
FA_BLOCK_K = 512
FA_BLOCK_K_MAIN = 2048
LOG2E = 1.4426950408889634
NT_DIMS = (((1,), (1,)), ((), ()))


def _pick(n, target, mult):
    best = None
    for d in range(mult, min(n, target) + 1, mult):
        if n % d == 0:
            best = d
    return best or n


def _params(*sem):
    return pltpu.CompilerParams(dimension_semantics=sem, vmem_limit_bytes=VMEM_LIMIT_BYTES)


def _rms(x, g):
    return x * lax.rsqrt(jnp.mean(x * x, axis=-1, keepdims=True) + EPS) * g


def _rms_kernel(x_ref, g_ref, o_ref):
    o_ref[...] = _rms(x_ref[...].astype(F32), g_ref[...]).astype(o_ref.dtype)


def rmsnorm_rows(x, g, out_dtype, row0=0, nrows=None, bm_target=512):
    R, C = x.shape
    nrows = R - row0 if nrows is None else nrows
    bm = _pick(nrows, bm_target, 16)
    assert row0 % bm == 0
    off = row0 // bm
    return pl.pallas_call(
        _rms_kernel,
        grid=(nrows // bm,),
        in_specs=[pl.BlockSpec((bm, C), lambda i: (i + off, 0)),
                  pl.BlockSpec((1, C), lambda i: (0, 0))],
        out_specs=pl.BlockSpec((bm, C), lambda i: (i, 0)),
        out_shape=jax.ShapeDtypeStruct((nrows, C), out_dtype),
        compiler_params=_params("parallel"),
        name="rmsnorm",
    )(x, g.reshape(1, C).astype(F32))


def _mm_kernel(a_ref, w_ref, o_ref):
    o_ref[...] = jnp.dot(a_ref[...], w_ref[...], preferred_element_type=F32).astype(o_ref.dtype)


def _mm_res_kernel(a_ref, w_ref, r_ref, o_ref):
    acc = jnp.dot(a_ref[...], w_ref[...], preferred_element_type=F32)
    o_ref[...] = (r_ref[...] + acc).astype(o_ref.dtype)


def _wspec(w, layer, bn, kb=None, ks=0):
    kb = w.shape[-2] if kb is None else kb
    if w.ndim == 3:
        return pl.BlockSpec((None, kb, bn), lambda i, j: (layer, ks, j))
    return pl.BlockSpec((kb, bn), lambda i, j: (ks, j))


def matmul(a, w, out_dtype, res=None, layer=None, bm_target=MM_ROW_TILE, bn_target=1024, k_splits=1,
           name="matmul"):
    if k_splits > 1:
        assert res is not None and a.shape[1] % (k_splits * LANE) == 0
        for ks in range(k_splits):
            res = _matmul_call(a, w, out_dtype, res, layer, bm_target, bn_target, a.shape[1] // k_splits, ks, name)
        return res
    return _matmul_call(a, w, out_dtype, res, layer, bm_target, bn_target, a.shape[1], 0, name)


def _matmul_call(a, w, out_dtype, res, layer, bm_target, bn_target, K, ks, name):
    M = a.shape[0]
    N = w.shape[-1]
    bm = _pick(M, bm_target, 16)
    bn = _pick(N, bn_target, LANE)
    in_specs = [pl.BlockSpec((bm, K), lambda i, j: (i, ks)), _wspec(w, layer, bn, K, ks)]
    args = [a, w]
    body = _mm_kernel
    if res is not None:
        in_specs.append(pl.BlockSpec((bm, bn), lambda i, j: (i, j)))
        args.append(res)
        body = _mm_res_kernel
    return pl.pallas_call(
        body,
        grid=(M // bm, N // bn),
        in_specs=in_specs,
        out_specs=pl.BlockSpec((bm, bn), lambda i, j: (i, j)),
        out_shape=jax.ShapeDtypeStruct((M, N), out_dtype),
        compiler_params=_params("parallel", "parallel"),
        name=name,
    )(*args)


def _win_kernel(w_ref, o_ref, *, kr0, half, g0):
    x = w_ref[...]
    n_gate = o_ref.shape[1] - g0
    o_ref[:, 0:kr0] = x[:, 0:kr0].astype(o_ref.dtype)
    t = x[:, kr0:kr0 + LANE]
    lane = lax.broadcasted_iota(jnp.int32, t.shape, 1)
    dup = jnp.where(lane < 2 * half, t,
                    jnp.where(lane < 3 * half, pltpu.roll(t, half, 1), pltpu.roll(t, 3 * half, 1)))
    o_ref[:, kr0:kr0 + LANE] = dup.astype(o_ref.dtype)
    o_ref[:, kr0 + LANE:g0] = jnp.zeros((x.shape[0], g0 - kr0 - LANE), o_ref.dtype)
    o_ref[:, g0:] = x[:, kr0 + 2 * half:kr0 + 2 * half + n_gate].astype(o_ref.dtype)


def relayout_w_in(w_in, kr0, half, g0, n_gate):
    DEPTH, K, NC = w_in.shape
    assert kr0 % LANE == 0 and 4 * half == LANE and kr0 + 2 * half + n_gate == NC
    tk = _pick(K, 128, 16)
    return pl.pallas_call(
        functools.partial(_win_kernel, kr0=kr0, half=half, g0=g0),
        grid=(DEPTH, K // tk),
        in_specs=[pl.BlockSpec((None, tk, NC), lambda l, i: (l, i, 0))],
        out_specs=pl.BlockSpec((None, tk, g0 + n_gate), lambda l, i: (l, i, 0)),
        out_shape=jax.ShapeDtypeStruct((DEPTH, K, g0 + n_gate), BF16),
        compiler_params=_params("parallel", "parallel"),
        name="relayout_w_in",
    )(w_in)


def _mm_q_kernel(a_ref, w_ref, t_ref, o_ref, *, heads, scale):
    acc = jnp.dot(a_ref[...], w_ref[...], preferred_element_type=F32)
    t = t_ref[...]
    for h in range(heads):
        c0 = 2 * LANE * h
        o_ref[:, c0:c0 + LANE] = (acc[:, c0:c0 + LANE] * scale).astype(o_ref.dtype)
        o_ref[:, c0 + LANE:c0 + 2 * LANE] = (acc[:, c0 + LANE:c0 + 2 * LANE] * t).astype(o_ref.dtype)


def matmul_q(a, w, layer, tq, scale, heads_per_tile=4):
    M, K = a.shape
    N = w.shape[-1]
    bm = _pick(M, MM_ROW_TILE, 16)
    bn = 2 * LANE * heads_per_tile
    assert N % bn == 0
    return pl.pallas_call(
        functools.partial(_mm_q_kernel, heads=heads_per_tile, scale=scale),
        grid=(M // bm, N // bn),
        in_specs=[pl.BlockSpec((bm, K), lambda i, j: (i, 0)),
                  _wspec(w, layer, bn),
                  pl.BlockSpec((bm, LANE), lambda i, j: (i, 0))],
        out_specs=pl.BlockSpec((bm, bn), lambda i, j: (i, j)),
        out_shape=jax.ShapeDtypeStruct((M, N), BF16),
        compiler_params=_params("parallel", "parallel"),
        name="matmul_q",
    )(a, w, tq)


def _mm_kv_kernel(a_ref, wk_ref, wv_ref, kr_ref, k_ref, v_ref, *, heads):
    a = a_ref[...]
    kn = jnp.dot(a, wk_ref[...], preferred_element_type=F32)
    vn = jnp.dot(a, wv_ref[...], preferred_element_type=F32)
    kr = kr_ref[...]
    ones = jnp.ones(kr.shape, v_ref.dtype)
    for h in range(heads):
        lo, mid, hi = 2 * LANE * h, 2 * LANE * h + LANE, 2 * LANE * (h + 1)
        k_ref[:, lo:mid] = kn[:, LANE * h:LANE * (h + 1)].astype(k_ref.dtype)
        k_ref[:, mid:hi] = kr
        v_ref[:, lo:mid] = vn[:, LANE * h:LANE * (h + 1)].astype(v_ref.dtype)
        v_ref[:, mid:hi] = ones


def matmul_kv(ckv_b, krc_b, w_uk, w_uv, layer, S, heads_per_tile=4):
    C = ckv_b.shape[1]
    N = w_uk.shape[-1]
    bm = _pick(S, 1024, 16)
    bn = LANE * heads_per_tile
    assert N % bn == 0
    return pl.pallas_call(
        functools.partial(_mm_kv_kernel, heads=heads_per_tile),
        grid=(S // bm, N // bn),
        in_specs=[pl.BlockSpec((bm, C), lambda i, j: (i, 0)),
                  _wspec(w_uk, layer, bn),
                  _wspec(w_uv, layer, bn),
                  pl.BlockSpec((bm, LANE), lambda i, j: (i, 0))],
        out_specs=[pl.BlockSpec((bm, 2 * bn), lambda i, j: (i, j)),
                   pl.BlockSpec((bm, 2 * bn), lambda i, j: (i, j))],
        out_shape=[jax.ShapeDtypeStruct((S, 2 * N), BF16),
                   jax.ShapeDtypeStruct((S, 2 * N), BF16)],
        compiler_params=_params("parallel", "parallel"),
        name="matmul_kv",
    )(ckv_b, w_uk, w_uv, krc_b)


def _sigmoid(x):
    return 1.0 / (1.0 + jnp.exp(-x))


def _mm_merge_kernel(a1_ref, w1_ref, a2_ref, w2_ref, ga_ref, gb_ref, o_ref):
    ya = jnp.dot(a1_ref[...], w1_ref[...], preferred_element_type=F32)
    yb = jnp.dot(a2_ref[...], w2_ref[...], preferred_element_type=F32)
    o_ref[...] = (_sigmoid(ga_ref[...]) * ya + _sigmoid(gb_ref[...]) * yb).astype(o_ref.dtype)


def matmul_merge(ya_in, w_out_a, ob, w_o_mla, layer, proj, ga_col0, gb_col0):
    M, K1 = ya_in.shape
    K2 = ob.shape[1]
    N = w_out_a.shape[-1]
    bm = _pick(M, MM_ROW_TILE, 16)
    bn = _pick(N, 512, LANE)
    assert ga_col0 % bn == 0 and gb_col0 % bn == 0
    ja, jb = ga_col0 // bn, gb_col0 // bn
    return pl.pallas_call(
        _mm_merge_kernel,
        grid=(M // bm, N // bn),
        in_specs=[pl.BlockSpec((bm, K1), lambda i, j: (i, 0)),
                  _wspec(w_out_a, layer, bn),
                  pl.BlockSpec((bm, K2), lambda i, j: (i, 0)),
                  _wspec(w_o_mla, layer, bn),
                  pl.BlockSpec((bm, bn), lambda i, j: (i, j + ja)),
                  pl.BlockSpec((bm, bn), lambda i, j: (i, j + jb))],
        out_specs=pl.BlockSpec((bm, bn), lambda i, j: (i, j)),
        out_shape=jax.ShapeDtypeStruct((M, N), BF16),
        compiler_params=_params("parallel", "parallel"),
        name="matmul_merge",
    )(ya_in, w_out_a, ob, w_o_mla, proj, proj)


def _conv3(u, prev8, w):
    rows, C = u.shape
    u3 = jnp.concatenate([prev8, u], axis=0).reshape(rows // SUBLANE + 1, SUBLANE, C)
    sub = lax.broadcasted_iota(jnp.int32, (1, SUBLANE, 1), 1)
    r1 = pltpu.roll(u3, 1, 1)
    r2 = pltpu.roll(u3, 2, 1)
    p1 = jnp.where(sub < 1, r1[:-1], r1[1:])
    p2 = jnp.where(sub < 2, r2[:-1], r2[1:])
    y = w[0:1][None] * p2 + w[1:2][None] * p1 + w[2:3][None] * u3[1:]
    return y.reshape(rows, C)


def _halo(prev_u, st_ref, first):
    st = st_ref[...]
    sub = lax.broadcasted_iota(jnp.int32, (HALO_ROWS, 1), 0)
    st8 = jnp.where(sub == HALO_ROWS - 2, st[0:1], jnp.where(sub == HALO_ROWS - 1, st[1:2], 0.0))
    if prev_u is None:
        return st8
    return jnp.where(first, st8, prev_u)


def _postin_kernel(*refs, has_prev):
    if has_prev:
        (b_ref, c_ref, h_ref, cp_ref, hp_ref, ql_ref, kv_ref, kr_ref, st_ref, taps_ref, qn_ref, kvn_ref,
         t_ref, ya_ref, cq_ref, ckv_ref, ckvb_ref, krc_ref, krcb_ref, ut_ref) = refs
        prev_u = cp_ref[...] * hp_ref[...]
    else:
        (b_ref, c_ref, h_ref, ql_ref, kv_ref, kr_ref, st_ref, taps_ref, qn_ref, kvn_ref,
         t_ref, ya_ref, cq_ref, ckv_ref, ckvb_ref, krc_ref, krcb_ref, ut_ref) = refs
        prev_u = None
    u = c_ref[...] * h_ref[...]
    prev8 = _halo(prev_u, st_ref, pl.program_id(0) == 0)
    ya_ref[...] = (b_ref[...] * _conv3(u, prev8, taps_ref[...])).astype(ya_ref.dtype)
    ut_ref[...] = u[u.shape[0] - HALO_ROWS:, :]
    cq_ref[...] = _rms(ql_ref[...], qn_ref[...]).astype(cq_ref.dtype)
    ckv = _rms(kv_ref[...], kvn_ref[...])
    ckv_ref[...] = ckv
    ckvb_ref[...] = ckv.astype(ckvb_ref.dtype)
    t = kr_ref[...] * t_ref[...]
    krc = t + pltpu.roll(t, LANE // 2, 1)
    krc_ref[...] = krc
    krcb_ref[...] = krc.astype(krcb_ref.dtype)


def post_in(proj, lay, state, taps, q_norm, kv_norm, tk, S, Bs, T, prev_outs=None):
    R = proj.shape[0]
    A, Q, C = lay["A"], lay["Q"], lay["C"]
    prompt = prev_outs is None
    bm = _pick(S, EW_ROW_TILE, 16) if prompt else T
    assert bm % 16 == 0 and S % bm == 0
    off = 0 if prompt else S // bm
    steps = S // bm if prompt else Bs
    rb = bm // HALO_ROWS

    def col(width, col0):
        assert col0 % width == 0
        return pl.BlockSpec((bm, width), lambda i, c=col0 // width: (i + off, c))

    in_specs = [col(A, lay["b"]), col(A, lay["c"]), col(A, lay["h"])]
    args = [proj, proj, proj]
    if prompt:
        for name in ("c", "h"):
            in_specs.append(pl.BlockSpec((HALO_ROWS, A),
                                         lambda i, c=lay[name] // A: (jnp.maximum(i * rb - 1, 0), c)))
            args.append(proj)
    in_specs += [col(Q, lay["q"]), col(C, lay["kv"]), col(LANE, lay["kr"]),
                 pl.BlockSpec((None, 2, A), (lambda i: (0, 0, 0)) if prompt else (lambda i: (i, 0, 0))),
                 pl.BlockSpec((3, A), lambda i: (0, 0)),
                 pl.BlockSpec((1, Q), lambda i: (0, 0)),
                 pl.BlockSpec((1, C), lambda i: (0, 0)),
                 pl.BlockSpec((bm, LANE), lambda i: (i + off, 0))]
    args += [proj, proj, proj, state, taps, q_norm.reshape(1, Q), kv_norm.reshape(1, C), tk]
    out_widths = [(A, BF16), (Q, BF16), (C, F32), (C, BF16), (LANE, F32), (LANE, BF16)]
    out_specs = [pl.BlockSpec((bm, w), lambda i: (i + off, 0)) for w, _ in out_widths]
    out_shape = [jax.ShapeDtypeStruct((R, w), dt) for w, dt in out_widths]
    n_pt = S // _pick(S, EW_ROW_TILE, 16)
    toff = 0 if prompt else n_pt
    out_specs.append(pl.BlockSpec((HALO_ROWS, A), lambda i: (i + toff, 0)))
    out_shape.append(jax.ShapeDtypeStruct(((n_pt + Bs) * HALO_ROWS, A), F32))
    aliases = {}
    if not prompt:
        n_in = len(args)
        for k, arr in enumerate(prev_outs):
            in_specs.append(pl.BlockSpec(memory_space=pl.ANY))
            args.append(arr)
            aliases[n_in + k] = k
    return pl.pallas_call(
        functools.partial(_postin_kernel_aliased if not prompt else _postin_kernel, has_prev=prompt),
        grid=(steps,),
        in_specs=in_specs,
        out_specs=out_specs,
        out_shape=out_shape,
        input_output_aliases=aliases,
        compiler_params=_params("arbitrary"),
        name="post_in_prompt" if prompt else "post_in_sample",
    )(*args)


def _postin_kernel_aliased(*refs, has_prev):
    n_out = 7
    n_in = len(refs) - 2 * n_out
    _postin_kernel(*refs[:n_in], *refs[n_in + n_out:], has_prev=has_prev)


def _ffn_kernel(*refs, has_prev, aliased):
    if aliased:
        refs = refs[:-2] + refs[-1:]
    if has_prev:
        a_ref, g_ref, ap_ref, gp_ref, sa_ref, sg_ref, wa_ref, wg_ref, o_ref = refs
        pa, pg = ap_ref[...], gp_ref[...]
    else:
        a_ref, g_ref, sa_ref, sg_ref, wa_ref, wg_ref, o_ref = refs
        pa = pg = None
    first = pl.program_id(0) == 0
    a = _conv3(a_ref[...], _halo(pa, sa_ref, first), wa_ref[...])
    g = _conv3(g_ref[...], _halo(pg, sg_ref, first), wg_ref[...])
    o_ref[...] = (g * _sigmoid(g) * a).astype(o_ref.dtype)


def ffn_gate(up, state, taps, S, Bs, T, prev_out=None):
    R, F2 = up.shape
    F = F2 // 2
    prompt = prev_out is None
    bm = _pick(S, EW_ROW_TILE, 16) if prompt else T
    bc = _pick(F, 5504, LANE)
    nc = F // bc
    off = 0 if prompt else S // bm
    steps = S // bm if prompt else Bs
    rb = bm // HALO_ROWS
    in_specs = [pl.BlockSpec((bm, bc), lambda i, j: (i + off, j)),
                pl.BlockSpec((bm, bc), lambda i, j: (i + off, j + nc))]
    args = [up, up]
    if prompt:
        in_specs += [pl.BlockSpec((HALO_ROWS, bc), lambda i, j: (jnp.maximum(i * rb - 1, 0), j)),
                     pl.BlockSpec((HALO_ROWS, bc), lambda i, j: (jnp.maximum(i * rb - 1, 0), j + nc))]
        args += [up, up]
    bsel = (lambda i: 0) if prompt else (lambda i: i)
    in_specs += [pl.BlockSpec((None, 2, bc), lambda i, j: (bsel(i), 0, j)),
                 pl.BlockSpec((None, 2, bc), lambda i, j: (bsel(i), 0, j + nc)),
                 pl.BlockSpec((3, bc), lambda i, j: (0, j)),
                 pl.BlockSpec((3, bc), lambda i, j: (0, j + nc))]
    args += [state, state, taps, taps]
    aliases = {}
    if not prompt:
        in_specs.append(pl.BlockSpec(memory_space=pl.ANY))
        args.append(prev_out)
        aliases[len(args) - 1] = 0
    return pl.pallas_call(
        functools.partial(_ffn_kernel, has_prev=prompt, aliased=not prompt),
        grid=(steps, nc),
        in_specs=in_specs,
        out_specs=pl.BlockSpec((bm, bc), lambda i, j: (i + off, j)),
        out_shape=jax.ShapeDtypeStruct((R, F), BF16),
        input_output_aliases=aliases,
        compiler_params=_params("arbitrary", "arbitrary"),
        name="ffn_gate_prompt" if prompt else "ffn_gate_sample",
    )(*args)


def _fa_kernel(q_ref, k_ref, v_ref, o_ref, m_sc, acc_sc, *, bq, bk, bk_main, sub):
    i = pl.program_id(1)
    m_sc[...] = jnp.full(m_sc.shape, NEG_BIG, F32)
    acc_sc[...] = jnp.zeros(acc_sc.shape, F32)
    chains = bq // sub

    def chain_block(r, k0, bk, local_k0):
        lane_tiles = bk // LANE
        rows = slice(sub * r, sub * (r + 1))
        kj = k_ref[pl.ds(k0, bk), :]
        vj = v_ref[pl.ds(k0, bk), :]
        s = lax.dot_general(q_ref[rows, :], kj, NT_DIMS, preferred_element_type=F32)
        if local_k0 is not None:
            kchunk = (lax.broadcasted_iota(jnp.int32, (sub, bk), 1) + local_k0) // CHUNK
            qchunk = (lax.broadcasted_iota(jnp.int32, (sub, bk), 0) + sub * r) // CHUNK
            s = jnp.where(kchunk <= qchunk, s, NEG_BIG)
        cols = [s[:, LANE * c:LANE * (c + 1)] for c in range(lane_tiles)]
        m_cur = cols[0]
        for c in cols[1:]:
            m_cur = jnp.maximum(m_cur, c)
        m_prev = m_sc[rows, :]
        m_new = jnp.maximum(m_prev, jnp.max(m_cur, axis=-1, keepdims=True))
        alpha = jnp.exp2(m_prev - m_new)
        p = jnp.concatenate([jnp.exp2(c - m_new) for c in cols], axis=1).astype(vj.dtype)
        pv = jnp.dot(p, vj, preferred_element_type=F32)
        acc_sc[rows, :] = jnp.concatenate([alpha, alpha], axis=1) * acc_sc[rows, :] + pv
        m_sc[rows, :] = m_new

    def body(j, carry):
        k0 = pl.multiple_of(j * bk_main, bk_main)
        for b in range(bk_main // bk):
            for r in range(chains):
                chain_block(r, k0 + b * bk, bk, None)
        return carry

    lax.fori_loop(0, i * (bq // bk_main), body, 0)
    t0 = pl.multiple_of(i * bq, bq)
    for b in range(bq // bk):
        for r in range(chains):
            if b * bk < (r + 1) * sub:
                chain_block(r, t0 + b * bk, bk, None if (b + 1) * bk <= r * sub else b * bk)
    acc = acc_sc[...]
    o_ref[...] = (acc[:, :LANE] / acc[:, LANE:]).astype(o_ref.dtype)


def prompt_attention(q_pad, k_pad, v_ext, S, R, H):
    bq = _pick(S, FA_BLOCK_Q, CHUNK)
    sub = _pick(bq, FA_SUB_ROWS, CHUNK)
    bk = _pick(sub, FA_BLOCK_K, LANE)
    bk_main = _pick(bq, FA_BLOCK_K_MAIN, bk)
    assert bq % bk_main == 0 and sub % bk == 0 and S % bq == 0
    return pl.pallas_call(
        functools.partial(_fa_kernel, bq=bq, bk=bk, bk_main=bk_main, sub=sub),
        grid=(H, S // bq),
        in_specs=[pl.BlockSpec((bq, 2 * LANE), lambda h, i: (i, h)),
                  pl.BlockSpec((S, 2 * LANE), lambda h, i: (0, h)),
                  pl.BlockSpec((S, 2 * LANE), lambda h, i: (0, h))],
        out_specs=pl.BlockSpec((bq, LANE), lambda h, i: (i, h)),
        out_shape=jax.ShapeDtypeStruct((R, H * LANE), BF16),
        scratch_shapes=[pltpu.VMEM((bq, LANE), F32), pltpu.VMEM((bq, 2 * LANE), F32)],
        compiler_params=_params("parallel", "arbitrary"),
        name="prompt_attention",
    )(q_pad, k_pad, v_ext)


def _sattn_kernel(q_ref, wuk_ref, wuv_ref, cckv_ref, ckr_ref, nckv_ref, nkr_ref, ob_ref, o_ref,
                  ql_sc, ka_sc, *, H, T, C, P, KPAD):
    del ob_ref
    for h in range(H):
        qn = q_ref[:, 2 * LANE * h:2 * LANE * h + LANE]
        qlat = lax.dot_general(qn, wuk_ref[:, LANE * h:LANE * (h + 1)], NT_DIMS, preferred_element_type=F32)
        ql_sc[T * h:T * (h + 1), 0:C] = qlat.astype(ql_sc.dtype)
        ql_sc[T * h:T * (h + 1), C:C + LANE] = q_ref[:, 2 * LANE * h + LANE:2 * LANE * (h + 1)]
    ka_sc[0:P, 0:C] = cckv_ref[...].astype(ka_sc.dtype)
    ka_sc[0:P, C:C + LANE] = ckr_ref[...]
    ka_sc[P:P + T, 0:C] = nckv_ref[...]
    ka_sc[P:P + T, C:C + LANE] = nkr_ref[...]
    ka_sc[P + T:P + KPAD, :] = jnp.zeros((KPAD - T, C + LANE), ka_sc.dtype)
    ka = ka_sc[...]
    s = lax.dot_general(ql_sc[...], ka, NT_DIMS, preferred_element_type=F32)
    kpos = lax.broadcasted_iota(jnp.int32, s.shape, 1)
    qpos = P + lax.broadcasted_iota(jnp.int32, s.shape, 0) % T
    s = jnp.where((kpos < P + T) & (kpos // CHUNK <= qpos // CHUNK), s, NEG_BIG)
    m = jnp.max(s, axis=-1, keepdims=True)
    p = jnp.exp2(s - m)
    l = jnp.sum(p, axis=-1, keepdims=True)
    olat = (jnp.dot(p.astype(ka.dtype), ka[:, 0:C], preferred_element_type=F32) / l).astype(wuv_ref.dtype)
    for h in range(H):
        oh = jnp.dot(olat[T * h:T * (h + 1), :], wuv_ref[:, LANE * h:LANE * (h + 1)], preferred_element_type=F32)
        o_ref[:, LANE * h:LANE * (h + 1)] = oh.astype(o_ref.dtype)


def sample_attention(qfull, w_uk, w_uv, cache_ckv, cache_krd, layer, ckv_b, krc_b, ob, S, Bs, T, H):
    C = w_uk.shape[-2]
    P = cache_ckv.shape[2]
    KPAD = LANE
    assert T % 16 == 0 and S % T == 0 and T <= KPAD and w_uk.ndim == 3
    off = S // T
    return pl.pallas_call(
        functools.partial(_sattn_kernel, H=H, T=T, C=C, P=P, KPAD=KPAD),
        grid=(Bs,),
        in_specs=[pl.BlockSpec((T, 2 * LANE * H), lambda b: (b + off, 0)),
                  pl.BlockSpec((None, C, LANE * H), lambda b: (layer, 0, 0)),
                  pl.BlockSpec((None, C, LANE * H), lambda b: (layer, 0, 0)),
                  pl.BlockSpec((None, None, P, C), lambda b: (layer, b, 0, 0)),
                  pl.BlockSpec((None, None, P, LANE), lambda b: (layer, b, 0, 0)),
                  pl.BlockSpec((T, C), lambda b: (b + off, 0)),
                  pl.BlockSpec((T, LANE), lambda b: (b + off, 0)),
                  pl.BlockSpec(memory_space=pl.ANY)],
        out_specs=pl.BlockSpec((T, LANE * H), lambda b: (b + off, 0)),
        out_shape=jax.ShapeDtypeStruct(ob.shape, ob.dtype),
        scratch_shapes=[pltpu.VMEM((H * T, C + LANE), BF16), pltpu.VMEM((P + KPAD, C + LANE), BF16)],
        input_output_aliases={7: 0},
        compiler_params=_params("arbitrary"),
        name="sample_attention",
    )(qfull, w_uk, w_uv, cache_ckv, cache_krd, ckv_b, krc_b, ob)


def _xattn_kernel(*refs, heads, dh, scale, aliased):
    if aliased:
        q_ref, k_ref, v_ref, _, o_ref = refs
    else:
        q_ref, k_ref, v_ref, o_ref = refs
    for h in range(heads):
        sl = slice(dh * h, dh * (h + 1))
        kh = k_ref[:, sl].astype(BF16)
        vh = v_ref[:, sl].astype(BF16)
        s = lax.dot_general(q_ref[:, sl], kh, NT_DIMS, preferred_element_type=F32) * scale
        p = jnp.exp(s - jnp.max(s, axis=-1, keepdims=True))
        l = jnp.sum(p, axis=-1, keepdims=True)
        o_ref[:, sl] = (jnp.dot(p.astype(BF16), vh, preferred_element_type=F32) / l).astype(o_ref.dtype)


def cross_attention(qm, mem_k, mem_v, heads, S, Bs, T, layer=None, prev_out=None):
    R, W = qm.shape
    dh = W // heads
    prompt = prev_out is None
    bm = _pick(S, 512, 16) if prompt else T
    off = 0 if prompt else S // bm
    steps = S // bm if prompt else Bs
    if prompt:
        M = mem_k.shape[0]
        kv_spec = pl.BlockSpec((M, W), lambda i: (0, 0))
    else:
        M = mem_k.shape[2]
        kv_spec = pl.BlockSpec((None, None, M, W), lambda i: (layer, i, 0, 0))
    in_specs = [pl.BlockSpec((bm, W), lambda i: (i + off, 0)), kv_spec, kv_spec]
    args = [qm, mem_k, mem_v]
    aliases = {}
    if not prompt:
        in_specs.append(pl.BlockSpec(memory_space=pl.ANY))
        args.append(prev_out)
        aliases[3] = 0
    return pl.pallas_call(
        functools.partial(_xattn_kernel, heads=heads, dh=dh, scale=dh ** -0.5, aliased=not prompt),
        grid=(steps,),
        in_specs=in_specs,
        out_specs=pl.BlockSpec((bm, W), lambda i: (i + off, 0)),
        out_shape=jax.ShapeDtypeStruct((R, W), BF16),
        input_output_aliases=aliases,
        compiler_params=_params("arbitrary"),
        name="xattn_prompt" if prompt else "xattn_sample",
    )(*args)


def _rope_tables(S, Bs, T, P, half):
    pos = jnp.concatenate([jnp.arange(S, dtype=jnp.int32),
                           jnp.tile(P + jnp.arange(T, dtype=jnp.int32), Bs)])
    inv_freq = ROPE_BASE ** (-jnp.arange(half, dtype=F32) / half)
    ang = pos.astype(F32)[:, None] * inv_freq[None, :]
    cos, sin = jnp.cos(ang), jnp.sin(ang)
    return jnp.concatenate([cos, cos, -sin, sin], axis=-1)


def _step(x_prompt, x_sample, cache_mla_ckv, cache_mla_krope, cache_mem_k, cache_mem_v,
          state_conv_a, state_conv_ffn, mem_prompt,
          norm_mix, w_in, q_norm, kv_norm, w_uq, w_ukv, conv_a, w_out_a, w_o_mla, w_o,
          norm_xattn, mem_norm, w_mq, w_mk, w_mv, w_mo, norm_ffn, w_up, conv_ffn, w_down,
          norm_final):
    Bp, S, D = x_prompt.shape
    Bs, T, _ = x_sample.shape
    DEPTH = w_in.shape[0]
    P = cache_mla_ckv.shape[2]
    C = cache_mla_ckv.shape[3]
    ROPE = cache_mla_krope.shape[3]
    half = ROPE // 2
    A = conv_a.shape[2]
    Q = q_norm.shape[1]
    H = w_ukv.shape[2]
    NOPE = w_uq.shape[2] // H - ROPE
    V = w_ukv.shape[3] - NOPE
    MEM = mem_prompt.shape[1]
    MH, MD = cache_mem_k.shape[3], cache_mem_k.shape[4]
    F = w_down.shape[1]
    assert Bp == 1 and NOPE == LANE and V == LANE and 4 * half == LANE
    R = S + Bs * T
    scale = (NOPE + ROPE) ** -0.5

    lay = {"A": A, "Q": Q, "C": C, "b": 0, "c": A, "h": 2 * A, "q": 3 * A, "kv": 3 * A + Q,
           "kr": 3 * A + Q + C}
    used = 3 * A + Q + C + LANE
    g0 = -(-used // 1024) * 1024
    lay["ga"], lay["gb"] = g0, g0 + D
    kr0 = 3 * A + Q + C

    x = jnp.concatenate([x_prompt.reshape(S, D), x_sample.reshape(Bs * T, D)], axis=0)
    tk = _rope_tables(S, Bs, T, P, half)
    qscale = scale * LOG2E
    tq = tk * qscale
    zeros_a = jnp.zeros((1, 2, A), F32)
    zeros_f = jnp.zeros((1, 2, 2 * F), F32)
    cache_krd = jnp.concatenate([cache_mla_krope, cache_mla_krope], axis=-1).astype(BF16)
    cmk = cache_mem_k.reshape(DEPTH, Bs, MEM, MH * MD)
    cmv = cache_mem_v.reshape(DEPTH, Bs, MEM, MH * MD)
    mem = mem_prompt.reshape(MEM, D)

    bf = lambda w: w.astype(BF16)
    w_in_b = relayout_w_in(w_in, kr0, half, g0, 2 * D)
    wq = w_uq.reshape(DEPTH, Q, H, NOPE + ROPE)
    x1, x2 = bf(wq[..., NOPE:NOPE + half]), bf(wq[..., NOPE + half:])
    w_uq_b = jnp.concatenate([bf(wq[..., :NOPE]), x1, x2, x2, x1], axis=-1).reshape(DEPTH, Q, H * 2 * LANE)
    w_uk_b = bf(w_ukv[..., :NOPE]).reshape(DEPTH, C, H * NOPE)
    w_uv_b = bf(w_ukv[..., NOPE:]).reshape(DEPTH, C, H * V)
    w_out_a_b, w_o_mla_b, w_o_b = bf(w_out_a), bf(w_o_mla), bf(w_o)
    w_mq_b, w_mk_b, w_mv_b, w_mo_b = bf(w_mq), bf(w_mk), bf(w_mv), bf(w_mo)
    w_up_b, w_down_b = bf(w_up), bf(w_down)

    outs = {k: [] for k in ("ckv_p", "kr_p", "mk", "mv", "ca_p", "cf_p", "ckv_s", "kr_s", "ca_s", "cf_s")}
    for l in range(DEPTH):
        xn = rmsnorm_rows(x, norm_mix[l], BF16)
        proj = matmul(xn, w_in_b, F32, layer=l, name="matmul_in")
        po = post_in(proj, lay, zeros_a, conv_a[l], q_norm[l], kv_norm[l], tk, S, Bs, T)
        ya_in, cq, ckv, ckv_b, krc, krc_b, utail = post_in(
            proj, lay, state_conv_a[l], conv_a[l], q_norm[l], kv_norm[l], tk, S, Bs, T, prev_outs=po)
        qfull = matmul_q(cq, w_uq_b, l, tq, qscale)
        k_pad, v_ext = matmul_kv(ckv_b, krc_b, w_uk_b, w_uv_b, l, S)
        ob = prompt_attention(qfull, k_pad, v_ext, S, R, H)
        ob = sample_attention(qfull, w_uk_b, w_uv_b, cache_mla_ckv, cache_krd, l, ckv_b, krc_b, ob, S, Bs, T, H)
        merged = matmul_merge(ya_in, w_out_a_b, ob, w_o_mla_b, l, proj, lay["ga"], lay["gb"])
        x = matmul(merged, w_o_b, F32, res=x, layer=l, bn_target=512, name="matmul_o")

        mn = rmsnorm_rows(mem, mem_norm[l], BF16)
        mk = matmul(mn, w_mk_b, F32, layer=l, name="matmul_mk")
        mv = matmul(mn, w_mv_b, F32, layer=l, name="matmul_mv")
        xc = rmsnorm_rows(x, norm_xattn[l], BF16)
        qm = matmul(xc, w_mq_b, BF16, layer=l, name="matmul_mq")
        om = cross_attention(qm, mk, mv, MH, S, Bs, T)
        om = cross_attention(qm, cmk, cmv, MH, S, Bs, T, layer=l, prev_out=om)
        x = matmul(om, w_mo_b, F32, res=x, layer=l, bn_target=512, name="matmul_mo")

        xf = rmsnorm_rows(x, norm_ffn[l], BF16)
        up = matmul(xf, w_up_b, F32, layer=l, bn_target=512, name="matmul_up")
        hmid = ffn_gate(up, zeros_f, conv_ffn[l], S, Bs, T)
        hmid = ffn_gate(up, state_conv_ffn[l], conv_ffn[l], S, Bs, T, prev_out=hmid)
        x = matmul(hmid, w_down_b, F32, res=x, layer=l, bn_target=512, k_splits=2, name="matmul_down")

        n_pt = S // _pick(S, EW_ROW_TILE, 16)
        outs["ckv_p"].append(ckv[:S].reshape(Bp, S, C))
        outs["kr_p"].append(krc[:S, :ROPE].reshape(Bp, S, ROPE))
        outs["mk"].append(mk.reshape(Bp, MEM, MH, MD))
        outs["mv"].append(mv.reshape(Bp, MEM, MH, MD))
        outs["ca_p"].append(utail[n_pt * HALO_ROWS - 2:n_pt * HALO_ROWS].reshape(Bp, 2, A))
        outs["cf_p"].append(up[S - 2:S].reshape(Bp, 2, 2 * F))
        outs["ckv_s"].append(ckv[S:].reshape(Bs, T, C))
        outs["kr_s"].append(krc[S:, :ROPE].reshape(Bs, T, ROPE))
        outs["ca_s"].append(utail[n_pt * HALO_ROWS:].reshape(Bs, HALO_ROWS, A)[:, HALO_ROWS - 2:])
        outs["cf_s"].append(up[S:].reshape(Bs, T, 2 * F)[:, T - 2:])

    y_prompt = rmsnorm_rows(x, norm_final, F32, row0=0, nrows=S).reshape(Bp, S, D)
    y_sample = rmsnorm_rows(x, norm_final, F32, row0=S, nrows=Bs * T).reshape(Bs, T, D)
    st = {k: jnp.stack(v) for k, v in outs.items()}
    return (y_prompt, y_sample, st["ckv_p"], st["kr_p"], st["mk"], st["mv"], st["ca_p"], st["cf_p"],
            st["ckv_s"], st["kr_s"], st["ca_s"], st["cf_s"])


def kernel(x_prompt, x_sample, cache_mla_ckv, cache_mla_krope, cache_mem_k, cache_mem_v, state_conv_a, state_conv_ffn, mem_prompt, norm_mix, w_in, q_norm, kv_norm, w_uq, w_ukv, conv_a, w_out_a, w_o_mla, w_o, norm_xattn, mem_norm, w_mq, w_mk, w_mv, w_mo, norm_ffn, w_up, conv_ffn, w_down, norm_final):
    return _step(x_prompt, x_sample, cache_mla_ckv, cache_mla_krope, cache_mem_k, cache_mem_v,
                 state_conv_a, state_conv_ffn, mem_prompt,
                 norm_mix, w_in, q_norm, kv_norm, w_uq, w_ukv, conv_a, w_out_a, w_o_mla, w_o,
                 norm_xattn, mem_norm, w_mq, w_mk, w_mv, w_mo, norm_ffn, w_up, conv_ffn, w_down,
                 norm_final)
```

```python
import functools

import jax
import jax.numpy as jnp
from jax import lax
from jax.experimental import pallas as pl
from jax.experimental.pallas import tpu as pltpu

F32 = jnp.float32
BF16 = jnp.bfloat16

EPS = 1e-6
CHUNK = 64
ROPE_BASE = 10000.0
NEG_BIG = -1e30

VMEM_LIMIT_BYTES = 60 * 1024 * 1024
LANE = 128
SUBLANE = 8
HALO_ROWS = 8
MM_ROW_TILE = 1040
EW_ROW_TILE = 256
FA_BLOCK_Q = 2048
FA_SUB_ROWS = 1024
FA_BLOCK_K = 512
FA_BLOCK_K_MAIN = 2048
LOG2E = 1.4426950408889634
NT_DIMS = (((1,), (1,)), ((), ()))


def _pick(n, target, mult):
    best = None
    for d in range(mult, min(n, target) + 1, mult):
        if n % d == 0:
            best = d
    return best or n


def _params(*sem):
    return pltpu.CompilerParams(dimension_semantics=sem, vmem_limit_bytes=VMEM_LIMIT_BYTES)


def _rms(x, g):
    return x * lax.rsqrt(jnp.mean(x * x, axis=-1, keepdims=True) + EPS) * g


def _rms_kernel(x_ref, g_ref, o_ref):
    o_ref[...] = _rms(x_ref[...].astype(F32), g_ref[...]).astype(o_ref.dtype)


def rmsnorm_rows(x, g, out_dtype, row0=0, nrows=None, bm_target=512):
    R, C = x.shape
    nrows = R - row0 if nrows is None else nrows
    bm = _pick(nrows, bm_target, 16)
    assert row0 % bm == 0
    off = row0 // bm
    return pl.pallas_call(
        _rms_kernel,
        grid=(nrows // bm,),
        in_specs=[pl.BlockSpec((bm, C), lambda i: (i + off, 0)),
                  pl.BlockSpec((1, C), lambda i: (0, 0))],
        out_specs=pl.BlockSpec((bm, C), lambda i: (i, 0)),
        out_shape=jax.ShapeDtypeStruct((nrows, C), out_dtype),
        compiler_params=_params("parallel"),
        name="rmsnorm",
    )(x, g.reshape(1, C).astype(F32))


def _scale_rows(acc, rinv):
    return jnp.concatenate([acc[:, LANE * c:LANE * (c + 1)] * rinv for c in range(acc.shape[1] // LANE)], axis=1)


def _mm_kernel(a_ref, w_ref, o_ref):
    o_ref[...] = jnp.dot(a_ref[...], w_ref[...], preferred_element_type=F32).astype(o_ref.dtype)


def _mm_rinv_kernel(a_ref, w_ref, rinv_ref, o_ref):
    acc = jnp.dot(a_ref[...], w_ref[...], preferred_element_type=F32)
    o_ref[...] = _scale_rows(acc, rinv_ref[...]).astype(o_ref.dtype)


def _mm_res_kernel(a_ref, w_ref, r_ref, o_ref):
    acc = jnp.dot(a_ref[...], w_ref[...], preferred_element_type=F32)
    o_ref[...] = (r_ref[...] + acc).astype(o_ref.dtype)


def _emit_row_stats(y, xb_ref, rinv_ref, j, nj):
    xb_ref[...] = y.astype(xb_ref.dtype)
    sq = y * y
    part = sq[:, 0:LANE]
    for c in range(1, y.shape[1] // LANE):
        part = part + sq[:, LANE * c:LANE * (c + 1)]

    @pl.when(j == 0)
    def _():
        rinv_ref[...] = part

    @pl.when(j > 0)
    def _():
        rinv_ref[...] += part

    @pl.when(j == nj - 1)
    def _():
        tot = jnp.sum(rinv_ref[...], axis=-1, keepdims=True)
        rinv_ref[...] = jnp.broadcast_to(lax.rsqrt(tot * (1.0 / (nj * y.shape[1])) + EPS), rinv_ref.shape)


def _mm_res_stats_kernel(a_ref, w_ref, r_ref, o_ref, xb_ref, rinv_ref, *, nj):
    y = r_ref[...] + jnp.dot(a_ref[...], w_ref[...], preferred_element_type=F32)
    o_ref[...] = y
    _emit_row_stats(y, xb_ref, rinv_ref, pl.program_id(1), nj)


def _row_stats_kernel(x_ref, xb_ref, rinv_ref):
    x = x_ref[...]
    xb_ref[...] = x.astype(xb_ref.dtype)
    rinv = lax.rsqrt(jnp.mean(x * x, axis=-1, keepdims=True) + EPS)
    rinv_ref[...] = jnp.broadcast_to(rinv, rinv_ref.shape)


def row_stats(x):
    R, C = x.shape
    bm = _pick(R, 512, 16)
    return pl.pallas_call(
        _row_stats_kernel,
        grid=(R // bm,),
        in_specs=[pl.BlockSpec((bm, C), lambda i: (i, 0))],
        out_specs=[pl.BlockSpec((bm, C), lambda i: (i, 0)), pl.BlockSpec((bm, LANE), lambda i: (i, 0))],
        out_shape=[jax.ShapeDtypeStruct((R, C), BF16), jax.ShapeDtypeStruct((R, LANE), F32)],
        compiler_params=_params("parallel"),
        name="row_stats",
    )(x)


def _wspec(w, layer, bn, kb=None, ks=0):
    kb = w.shape[-2] if kb is None else kb
    if w.ndim == 3:
        return pl.BlockSpec((None, kb, bn), lambda i, j: (layer, ks, j))
    return pl.BlockSpec((kb, bn), lambda i, j: (ks, j))


def matmul(a, w, out_dtype, res=None, rinv=None, stats=False, layer=None, bm_target=MM_ROW_TILE,
           bn_target=1024, k_splits=1, name="matmul"):
    if k_splits > 1:
        assert res is not None and rinv is None and a.shape[1] % (k_splits * LANE) == 0
        for ks in range(k_splits):
            last = ks == k_splits - 1
            res = _matmul_call(a, w, out_dtype, res, None, stats and last, layer, bm_target, bn_target,
                               a.shape[1] // k_splits, ks, name)
        return res
    return _matmul_call(a, w, out_dtype, res, rinv, stats, layer, bm_target, bn_target, a.shape[1], 0, name)


def _matmul_call(a, w, out_dtype, res, rinv, stats, layer, bm_target, bn_target, K, ks, name):
    M = a.shape[0]
    N = w.shape[-1]
    bm = _pick(M, bm_target, 16)
    bn = _pick(N, bn_target, LANE)
    in_specs = [pl.BlockSpec((bm, K), lambda i, j: (i, ks)), _wspec(w, layer, bn, K, ks)]
    args = [a, w]
    body = _mm_kernel
    out_specs = pl.BlockSpec((bm, bn), lambda i, j: (i, j))
    out_shape = jax.ShapeDtypeStruct((M, N), out_dtype)
    sem = ("parallel", "parallel")
    if res is not None:
        assert rinv is None
        in_specs.append(pl.BlockSpec((bm, bn), lambda i, j: (i, j)))
        args.append(res)
        body = _mm_res_kernel
        if stats:
            assert out_dtype == F32
            body = functools.partial(_mm_res_stats_kernel, nj=N // bn)
            out_specs = [out_specs, pl.BlockSpec((bm, bn), lambda i, j: (i, j)),
                         pl.BlockSpec((bm, LANE), lambda i, j: (i, 0))]
            out_shape = [out_shape, jax.ShapeDtypeStruct((M, N), BF16), jax.ShapeDtypeStruct((M, LANE), F32)]
            sem = ("parallel", "arbitrary")
    elif rinv is not None:
        in_specs.append(pl.BlockSpec((bm, LANE), lambda i, j: (i, 0)))
        args.append(rinv)
        body = _mm_rinv_kernel
    return pl.pallas_call(
        body,
        grid=(M // bm, N // bn),
        in_specs=in_specs,
        out_specs=out_specs,
        out_shape=out_shape,
        compiler_params=_params(*sem),
        name=name,
    )(*args)


def _mm_nt_rinv_kernel(a_ref, wt_ref, rinv_ref, o_ref):
    acc = lax.dot_general(a_ref[...], wt_ref[...], NT_DIMS, preferred_element_type=F32)
    o_ref[...] = _scale_rows(acc, rinv_ref[...]).astype(o_ref.dtype)


def matmul_nt(a, wt, rinv, out_dtype, layer, bn_target=1024, name="matmul_nt"):
    M, K = a.shape
    N = wt.shape[1]
    bm = _pick(M, MM_ROW_TILE, 16)
    bn = _pick(N, bn_target, LANE)
    return pl.pallas_call(
        _mm_nt_rinv_kernel,
        grid=(M // bm, N // bn),
        in_specs=[pl.BlockSpec((bm, K), lambda i, j: (i, 0)),
                  pl.BlockSpec((None, bn, K), lambda i, j: (layer, j, 0)),
                  pl.BlockSpec((bm, LANE), lambda i, j: (i, 0))],
        out_specs=pl.BlockSpec((bm, bn), lambda i, j: (i, j)),
        out_shape=jax.ShapeDtypeStruct((M, N), out_dtype),
        compiler_params=_params("parallel", "parallel"),
        name=name,
    )(a, wt, rinv)


def _win_kernel(w_ref, g_ref, o_ref, *, kr0, half, g0):
    n_gate = o_ref.shape[0] - g0
    dt = o_ref.dtype
    g = g_ref[...]
    o_ref[0:kr0, :] = (w_ref[0:kr0, :] * g).astype(dt)
    x1 = (w_ref[kr0:kr0 + half, :] * g).astype(dt)
    x2 = (w_ref[kr0 + half:kr0 + 2 * half, :] * g).astype(dt)
    o_ref[kr0:kr0 + half, :] = x1
    o_ref[kr0 + half:kr0 + 2 * half, :] = x2
    o_ref[kr0 + 2 * half:kr0 + 3 * half, :] = x2
    o_ref[kr0 + 3 * half:kr0 + 4 * half, :] = x1
    o_ref[kr0 + 4 * half:g0, :] = jnp.zeros((g0 - kr0 - 4 * half, o_ref.shape[1]), dt)
    o_ref[g0:, :] = (w_ref[kr0 + 2 * half:kr0 + 2 * half + n_gate, :] * g).astype(dt)


def relayout_w_in(w_in_t, gain, kr0, half, g0, n_gate):
    DEPTH, NC, K = w_in_t.shape
    assert kr0 % 16 == 0 and half % 16 == 0 and g0 % 16 == 0 and kr0 + 2 * half + n_gate == NC
    return pl.pallas_call(
        functools.partial(_win_kernel, kr0=kr0, half=half, g0=g0),
        grid=(DEPTH, K // LANE),
        in_specs=[pl.BlockSpec((None, NC, LANE), lambda l, i: (l, 0, i)),
                  pl.BlockSpec((None, 1, LANE), lambda l, i: (l, 0, i))],
        out_specs=pl.BlockSpec((None, g0 + n_gate, LANE), lambda l, i: (l, 0, i)),
        out_shape=jax.ShapeDtypeStruct((DEPTH, g0 + n_gate, K), BF16),
        compiler_params=_params("parallel", "parallel"),
        name="relayout_w_in",
    )(w_in_t, gain.reshape(DEPTH, 1, K))


def _mm_q_kernel(a_ref, w_ref, t_ref, o_ref, *, heads, scale):
    acc = jnp.dot(a_ref[...], w_ref[...], preferred_element_type=F32)
    t = t_ref[...]
    for h in range(heads):
        c0 = 2 * LANE * h
        o_ref[:, c0:c0 + LANE] = (acc[:, c0:c0 + LANE] * scale).astype(o_ref.dtype)
        o_ref[:, c0 + LANE:c0 + 2 * LANE] = (acc[:, c0 + LANE:c0 + 2 * LANE] * t).astype(o_ref.dtype)


def matmul_q(a, w, layer, tq, scale, heads_per_tile=4):
    M, K = a.shape
    N = w.shape[-1]
    bm = _pick(M, MM_ROW_TILE, 16)
    bn = 2 * LANE * heads_per_tile
    assert N % bn == 0
    return pl.pallas_call(
        functools.partial(_mm_q_kernel, heads=heads_per_tile, scale=scale),
        grid=(M // bm, N // bn),
        in_specs=[pl.BlockSpec((bm, K), lambda i, j: (i, 0)),
                  _wspec(w, layer, bn),
                  pl.BlockSpec((bm, LANE), lambda i, j: (i, 0))],
        out_specs=pl.BlockSpec((bm, bn), lambda i, j: (i, j)),
        out_shape=jax.ShapeDtypeStruct((M, N), BF16),
        compiler_params=_params("parallel", "parallel"),
        name="matmul_q",
    )(a, w, tq)


def _mm_kv_kernel(a_ref, wk_ref, wv_ref, kr_ref, k_ref, v_ref, *, heads):
    a = a_ref[...]
    kn = jnp.dot(a, wk_ref[...], preferred_element_type=F32)
    vn = jnp.dot(a, wv_ref[...], preferred_element_type=F32)
    kr = kr_ref[...]
    ones = jnp.ones(kr.shape, v_ref.dtype)
    for h in range(heads):
        lo, mid, hi = 2 * LANE * h, 2 * LANE * h + LANE, 2 * LANE * (h + 1)
        k_ref[:, lo:mid] = kn[:, LANE * h:LANE * (h + 1)].astype(k_ref.dtype)
        k_ref[:, mid:hi] = kr
        v_ref[:, lo:mid] = vn[:, LANE * h:LANE * (h + 1)].astype(v_ref.dtype)
        v_ref[:, mid:hi] = ones


def matmul_kv(ckv_b, krc_b, w_uk, w_uv, layer, S, heads_per_tile=4):
    C = ckv_b.shape[1]
    N = w_uk.shape[-1]
    bm = _pick(S, 1024, 16)
    bn = LANE * heads_per_tile
    assert N % bn == 0
    return pl.pallas_call(
        functools.partial(_mm_kv_kernel, heads=heads_per_tile),
        grid=(S // bm, N // bn),
        in_specs=[pl.BlockSpec((bm, C), lambda i, j: (i, 0)),
                  _wspec(w_uk, layer, bn),
                  _wspec(w_uv, layer, bn),
                  pl.BlockSpec((bm, LANE), lambda i, j: (i, 0))],
        out_specs=[pl.BlockSpec((bm, 2 * bn), lambda i, j: (i, j)),
                   pl.BlockSpec((bm, 2 * bn), lambda i, j: (i, j))],
        out_shape=[jax.ShapeDtypeStruct((S, 2 * N), BF16),
                   jax.ShapeDtypeStruct((S, 2 * N), BF16)],
        compiler_params=_params("parallel", "parallel"),
        name="matmul_kv",
    )(ckv_b, w_uk, w_uv, krc_b)


def _sigmoid(x):
    return 1.0 / (1.0 + jnp.exp(-x))


def _mm_merge_kernel(a1_ref, w1_ref, a2_ref, w2_ref, ga_ref, gb_ref, o_ref):
    ya = jnp.dot(a1_ref[...], w1_ref[...], preferred_element_type=F32)
    yb = jnp.dot(a2_ref[...], w2_ref[...], preferred_element_type=F32)
    o_ref[...] = (_sigmoid(ga_ref[...]) * ya + _sigmoid(gb_ref[...]) * yb).astype(o_ref.dtype)


def matmul_merge(ya_in, w_out_a, ob, w_o_mla, layer, proj, ga_col0, gb_col0):
    M, K1 = ya_in.shape
    K2 = ob.shape[1]
    N = w_out_a.shape[-1]
    bm = _pick(M, MM_ROW_TILE, 16)
    bn = _pick(N, 512, LANE)
    assert ga_col0 % bn == 0 and gb_col0 % bn == 0
    ja, jb = ga_col0 // bn, gb_col0 // bn
    return pl.pallas_call(
        _mm_merge_kernel,
        grid=(M // bm, N // bn),
        in_specs=[pl.BlockSpec((bm, K1), lambda i, j: (i, 0)),
                  _wspec(w_out_a, layer, bn),
                  pl.BlockSpec((bm, K2), lambda i, j: (i, 0)),
                  _wspec(w_o_mla, layer, bn),
                  pl.BlockSpec((bm, bn), lambda i, j: (i, j + ja)),
                  pl.BlockSpec((bm, bn), lambda i, j: (i, j + jb))],
        out_specs=pl.BlockSpec((bm, bn), lambda i, j: (i, j)),
        out_shape=jax.ShapeDtypeStruct((M, N), BF16),
        compiler_params=_params("parallel", "parallel"),
        name="matmul_merge",
    )(ya_in, w_out_a, ob, w_o_mla, proj, proj)


def _conv3(u, prev8, w):
    rows, C = u.shape
    u3 = jnp.concatenate([prev8, u], axis=0).reshape(rows // SUBLANE + 1, SUBLANE, C)
    sub = lax.broadcasted_iota(jnp.int32, (1, SUBLANE, 1), 1)
    r1 = pltpu.roll(u3, 1, 1)
    r2 = pltpu.roll(u3, 2, 1)
    p1 = jnp.where(sub < 1, r1[:-1], r1[1:])
    p2 = jnp.where(sub < 2, r2[:-1], r2[1:])
    y = w[0:1][None] * p2 + w[1:2][None] * p1 + w[2:3][None] * u3[1:]
    return y.reshape(rows, C)


def _halo(prev_u, st_ref, first):
    st = st_ref[...]
    sub = lax.broadcasted_iota(jnp.int32, (HALO_ROWS, 1), 0)
    st8 = jnp.where(sub == HALO_ROWS - 2, st[0:1], jnp.where(sub == HALO_ROWS - 1, st[1:2], 0.0))
    if prev_u is None:
        return st8
    return jnp.where(first, st8, prev_u)


def _postin_kernel(*refs, has_prev):
    if has_prev:
        (b_ref, c_ref, h_ref, cp_ref, hp_ref, ql_ref, kv_ref, kr_ref, st_ref, taps_ref, qn_ref, kvn_ref,
         t_ref, ya_ref, cq_ref, ckv_ref, ckvb_ref, krc_ref, krcb_ref, ut_ref) = refs
        prev_u = cp_ref[...] * hp_ref[...]
    else:
        (b_ref, c_ref, h_ref, ql_ref, kv_ref, kr_ref, st_ref, taps_ref, qn_ref, kvn_ref,
         t_ref, ya_ref, cq_ref, ckv_ref, ckvb_ref, krc_ref, krcb_ref, ut_ref) = refs
        prev_u = None
    u = c_ref[...] * h_ref[...]
    prev8 = _halo(prev_u, st_ref, pl.program_id(0) == 0)
    ya_ref[...] = (b_ref[...] * _conv3(u, prev8, taps_ref[...])).astype(ya_ref.dtype)
    ut_ref[...] = u[u.shape[0] - HALO_ROWS:, :]
    cq_ref[...] = _rms(ql_ref[...], qn_ref[...]).astype(cq_ref.dtype)
    ckv = _rms(kv_ref[...], kvn_ref[...])
    ckv_ref[...] = ckv
    ckvb_ref[...] = ckv.astype(ckvb_ref.dtype)
    t = kr_ref[...] * t_ref[...]
    krc = t + pltpu.roll(t, LANE // 2, 1)
    krc_ref[...] = krc
    krcb_ref[...] = krc.astype(krcb_ref.dtype)


def post_in(proj, lay, state, taps, q_norm, kv_norm, tk, S, Bs, T, prev_outs=None):
    R = proj.shape[0]
    A, Q, C = lay["A"], lay["Q"], lay["C"]
    prompt = prev_outs is None
    bm = _pick(S, EW_ROW_TILE, 16) if prompt else T
    assert bm % 16 == 0 and S % bm == 0
    off = 0 if prompt else S // bm
    steps = S // bm if prompt else Bs
    rb = bm // HALO_ROWS

    def col(width, col0):
        assert col0 % width == 0
        return pl.BlockSpec((bm, width), lambda i, c=col0 // width: (i + off, c))

    in_specs = [col(A, lay["b"]), col(A, lay["c"]), col(A, lay["h"])]
    args = [proj, proj, proj]
    if prompt:
        for name in ("c", "h"):
            in_specs.append(pl.BlockSpec((HALO_ROWS, A),
                                         lambda i, c=lay[name] // A: (jnp.maximum(i * rb - 1, 0), c)))
            args.append(proj)
    in_specs += [col(Q, lay["q"]), col(C, lay["kv"]), col(LANE, lay["kr"]),
                 pl.BlockSpec((None, 2, A), (lambda i: (0, 0, 0)) if prompt else (lambda i: (i, 0, 0))),
                 pl.BlockSpec((3, A), lambda i: (0, 0)),
                 pl.BlockSpec((1, Q), lambda i: (0, 0)),
                 pl.BlockSpec((1, C), lambda i: (0, 0)),
                 pl.BlockSpec((bm, LANE), lambda i: (i + off, 0))]
    args += [proj, proj, proj, state, taps, q_norm.reshape(1, Q), kv_norm.reshape(1, C), tk]
    out_widths = [(A, BF16), (Q, BF16), (C, F32), (C, BF16), (LANE, F32), (LANE, BF16)]
    out_specs = [pl.BlockSpec((bm, w), lambda i: (i + off, 0)) for w, _ in out_widths]
    out_shape = [jax.ShapeDtypeStruct((R, w), dt) for w, dt in out_widths]
    n_pt = S // _pick(S, EW_ROW_TILE, 16)
    toff = 0 if prompt else n_pt
    out_specs.append(pl.BlockSpec((HALO_ROWS, A), lambda i: (i + toff, 0)))
    out_shape.append(jax.ShapeDtypeStruct(((n_pt + Bs) * HALO_ROWS, A), F32))
    aliases = {}
    if not prompt:
        n_in = len(args)
        for k, arr in enumerate(prev_outs):
            in_specs.append(pl.BlockSpec(memory_space=pl.ANY))
            args.append(arr)
            aliases[n_in + k] = k
    return pl.pallas_call(
        functools.partial(_postin_kernel_aliased if not prompt else _postin_kernel, has_prev=prompt),
        grid=(steps,),
        in_specs=in_specs,
        out_specs=out_specs,
        out_shape=out_shape,
        input_output_aliases=aliases,
        compiler_params=_params("arbitrary"),
        name="post_in_prompt" if prompt else "post_in_sample",
    )(*args)


def _postin_kernel_aliased(*refs, has_prev):
    n_out = 7
    n_in = len(refs) - 2 * n_out
    _postin_kernel(*refs[:n_in], *refs[n_in + n_out:], has_prev=has_prev)


def _ffn_kernel(*refs, has_prev, aliased):
    if aliased:
        refs = refs[:-2] + refs[-1:]
    if has_prev:
        a_ref, g_ref, ap_ref, gp_ref, sa_ref, sg_ref, wa_ref, wg_ref, o_ref = refs
        pa, pg = ap_ref[...], gp_ref[...]
    else:
        a_ref, g_ref, sa_ref, sg_ref, wa_ref, wg_ref, o_ref = refs
        pa = pg = None
    first = pl.program_id(0) == 0
    a = _conv3(a_ref[...], _halo(pa, sa_ref, first), wa_ref[...])
    g = _conv3(g_ref[...], _halo(pg, sg_ref, first), wg_ref[...])
    o_ref[...] = (g * _sigmoid(g) * a).astype(o_ref.dtype)


def ffn_gate(up, state, taps, S, Bs, T, prev_out=None):
    R, F2 = up.shape
    F = F2 // 2
    prompt = prev_out is None
    bm = _pick(S, EW_ROW_TILE, 16) if prompt else T
    bc = _pick(F, 5504, LANE)
    nc = F // bc
    off = 0 if prompt else S // bm
    steps = S // bm if prompt else Bs
    rb = bm // HALO_ROWS
    in_specs = [pl.BlockSpec((bm, bc), lambda i, j: (i + off, j)),
                pl.BlockSpec((bm, bc), lambda i, j: (i + off, j + nc))]
    args = [up, up]
    if prompt:
        in_specs += [pl.BlockSpec((HALO_ROWS, bc), lambda i, j: (jnp.maximum(i * rb - 1, 0), j)),
                     pl.BlockSpec((HALO_ROWS, bc), lambda i, j: (jnp.maximum(i * rb - 1, 0), j + nc))]
        args += [up, up]
    bsel = (lambda i: 0) if prompt else (lambda i: i)
    in_specs += [pl.BlockSpec((None, 2, bc), lambda i, j: (bsel(i), 0, j)),
                 pl.BlockSpec((None, 2, bc), lambda i, j: (bsel(i), 0, j + nc)),
                 pl.BlockSpec((3, bc), lambda i, j: (0, j)),
                 pl.BlockSpec((3, bc), lambda i, j: (0, j + nc))]
    args += [state, state, taps, taps]
    aliases = {}
    if not prompt:
        in_specs.append(pl.BlockSpec(memory_space=pl.ANY))
        args.append(prev_out)
        aliases[len(args) - 1] = 0
    return pl.pallas_call(
        functools.partial(_ffn_kernel, has_prev=prompt, aliased=not prompt),
        grid=(steps, nc),
        in_specs=in_specs,
        out_specs=pl.BlockSpec((bm, bc), lambda i, j: (i + off, j)),
        out_shape=jax.ShapeDtypeStruct((R, F), BF16),
        input_output_aliases=aliases,
        compiler_params=_params("arbitrary", "arbitrary"),
        name="ffn_gate_prompt" if prompt else "ffn_gate_sample",
    )(*args)


def _fa_kernel(q_ref, k_ref, v_ref, o_ref, m_sc, acc_sc, *, bq, bk, bk_main, sub):
    i = pl.program_id(1)
    m_sc[...] = jnp.full(m_sc.shape, NEG_BIG, F32)
    acc_sc[...] = jnp.zeros(acc_sc.shape, F32)
    chains = bq // sub

    def chain_block(row0, nrows, k0, bk, local_k0):
        lane_tiles = bk // LANE
        rows = slice(row0, row0 + nrows)
        kj = k_ref[pl.ds(k0, bk), :]
        vj = v_ref[pl.ds(k0, bk), :]
        s = lax.dot_general(q_ref[rows, :], kj, NT_DIMS, preferred_element_type=F32)
        if local_k0 is not None:
            kchunk = (lax.broadcasted_iota(jnp.int32, (nrows, bk), 1) + local_k0) // CHUNK
            qchunk = (lax.broadcasted_iota(jnp.int32, (nrows, bk), 0) + row0) // CHUNK
            s = jnp.where(kchunk <= qchunk, s, NEG_BIG)
        cols = [s[:, LANE * c:LANE * (c + 1)] for c in range(lane_tiles)]
        m_cur = cols[0]
        for c in cols[1:]:
            m_cur = jnp.maximum(m_cur, c)
        m_prev = m_sc[rows, :]
        m_new = jnp.maximum(m_prev, jnp.max(m_cur, axis=-1, keepdims=True))
        alpha = jnp.exp2(m_prev - m_new)
        p = jnp.concatenate([jnp.exp2(c - m_new) for c in cols], axis=1).astype(vj.dtype)
        pv = jnp.dot(p, vj, preferred_element_type=F32)
        acc_sc[rows, :] = jnp.concatenate([alpha, alpha], axis=1) * acc_sc[rows, :] + pv
        m_sc[rows, :] = m_new

    def body(j, carry):
        k0 = pl.multiple_of(j * bk_main, bk_main)
        for b in range(bk_main // bk):
            for r in range(chains):
                chain_block(sub * r, sub, k0 + b * bk, bk, None)
        return carry

    lax.fori_loop(0, i * (bq // bk_main), body, 0)
    t0 = pl.multiple_of(i * bq, bq)
    for b in range(bq // bk):
        for r in range(chains):
            lo = max(sub * r, b * bk)
            if lo < sub * (r + 1):
                chain_block(lo, sub * (r + 1) - lo, t0 + b * bk, bk,
                            None if (b + 1) * bk <= sub * r else b * bk)
    acc = acc_sc[...]
    o_ref[...] = (acc[:, :LANE] / acc[:, LANE:]).astype(o_ref.dtype)


def prompt_attention(q_pad, k_pad, v_ext, S, R, H):
    bq = _pick(S, FA_BLOCK_Q, CHUNK)
    sub = _pick(bq, FA_SUB_ROWS, CHUNK)
    bk = _pick(sub, FA_BLOCK_K, LANE)
    bk_main = _pick(bq, FA_BLOCK_K_MAIN, bk)
    assert bq % bk_main == 0 and sub % bk == 0 and S % bq == 0
    return pl.pallas_call(
        functools.partial(_fa_kernel, bq=bq, bk=bk, bk_main=bk_main, sub=sub),
        grid=(H, S // bq),
        in_specs=[pl.BlockSpec((bq, 2 * LANE), lambda h, i: (i, h)),
                  pl.BlockSpec((S, 2 * LANE), lambda h, i: (0, h)),
                  pl.BlockSpec((S, 2 * LANE), lambda h, i: (0, h))],
        out_specs=pl.BlockSpec((bq, LANE), lambda h, i: (i, h)),
        out_shape=jax.ShapeDtypeStruct((R, H * LANE), BF16),
        scratch_shapes=[pltpu.VMEM((bq, LANE), F32), pltpu.VMEM((bq, 2 * LANE), F32)],
        compiler_params=_params("parallel", "arbitrary"),
        name="prompt_attention",
    )(q_pad, k_pad, v_ext)


def _sattn_kernel(q_ref, wuk_ref, wuv_ref, cckv_ref, ckr_ref, nckv_ref, nkr_ref, ob_ref, o_ref,
                  ql_sc, ka_sc, *, H, T, C, P, KPAD):
    del ob_ref
    for h in range(H):
        qn = q_ref[:, 2 * LANE * h:2 * LANE * h + LANE]
        qlat = lax.dot_general(qn, wuk_ref[:, LANE * h:LANE * (h + 1)], NT_DIMS, preferred_element_type=F32)
        ql_sc[T * h:T * (h + 1), 0:C] = qlat.astype(ql_sc.dtype)
        ql_sc[T * h:T * (h + 1), C:C + LANE] = q_ref[:, 2 * LANE * h + LANE:2 * LANE * (h + 1)]
    ka_sc[0:P, 0:C] = cckv_ref[...].astype(ka_sc.dtype)
    ka_sc[0:P, C:C + LANE] = ckr_ref[...]
    ka_sc[P:P + T, 0:C] = nckv_ref[...]
    ka_sc[P:P + T, C:C + LANE] = nkr_ref[...]
    ka_sc[P + T:P + KPAD, :] = jnp.zeros((KPAD - T, C + LANE), ka_sc.dtype)
    ka = ka_sc[...]
    s = lax.dot_general(ql_sc[...], ka, NT_DIMS, preferred_element_type=F32)
    kpos = lax.broadcasted_iota(jnp.int32, s.shape, 1)
    qpos = P + lax.broadcasted_iota(jnp.int32, s.shape, 0) % T
    s = jnp.where((kpos < P + T) & (kpos // CHUNK <= qpos // CHUNK), s, NEG_BIG)
    m = jnp.max(s, axis=-1, keepdims=True)
    p = jnp.exp2(s - m)
    l = jnp.sum(p, axis=-1, keepdims=True)
    olat = (jnp.dot(p.astype(ka.dtype), ka[:, 0:C], preferred_element_type=F32) / l).astype(wuv_ref.dtype)
    for h in range(H):
        oh = jnp.dot(olat[T * h:T * (h + 1), :], wuv_ref[:, LANE * h:LANE * (h + 1)], preferred_element_type=F32)
        o_ref[:, LANE * h:LANE * (h + 1)] = oh.astype(o_ref.dtype)


def sample_attention(qfull, w_uk, w_uv, cache_ckv, cache_krd, layer, ckv_b, krc_b, ob, S, Bs, T, H):
    C = w_uk.shape[-2]
    P = cache_ckv.shape[2]
    KPAD = LANE
    assert T % 16 == 0 and S % T == 0 and T <= KPAD and w_uk.ndim == 3
    off = S // T
    return pl.pallas_call(
        functools.partial(_sattn_kernel, H=H, T=T, C=C, P=P, KPAD=KPAD),
        grid=(Bs,),
        in_specs=[pl.BlockSpec((T, 2 * LANE * H), lambda b: (b + off, 0)),
                  pl.BlockSpec((None, C, LANE * H), lambda b: (layer, 0, 0)),
                  pl.BlockSpec((None, C, LANE * H), lambda b: (layer, 0, 0)),
                  pl.BlockSpec((None, None, P, C), lambda b: (layer, b, 0, 0)),
                  pl.BlockSpec((None, None, P, LANE), lambda b: (layer, b, 0, 0)),
                  pl.BlockSpec((T, C), lambda b: (b + off, 0)),
                  pl.BlockSpec((T, LANE), lambda b: (b + off, 0)),
                  pl.BlockSpec(memory_space=pl.ANY)],
        out_specs=pl.BlockSpec((T, LANE * H), lambda b: (b + off, 0)),
        out_shape=jax.ShapeDtypeStruct(ob.shape, ob.dtype),
        scratch_shapes=[pltpu.VMEM((H * T, C + LANE), BF16), pltpu.VMEM((P + KPAD, C + LANE), BF16)],
        input_output_aliases={7: 0},
        compiler_params=_params("arbitrary"),
        name="sample_attention",
    )(qfull, w_uk, w_uv, cache_ckv, cache_krd, ckv_b, krc_b, ob)


def _xattn_kernel(*refs, heads, dh, scale, aliased):
    if aliased:
        q_ref, k_ref, v_ref, _, o_ref = refs
    else:
        q_ref, k_ref, v_ref, o_ref = refs
    for h in range(heads):
        sl = slice(dh * h, dh * (h + 1))
        kh = k_ref[:, sl].astype(BF16)
        vh = v_ref[:, sl].astype(BF16)
        s = lax.dot_general(q_ref[:, sl], kh, NT_DIMS, preferred_element_type=F32) * scale
        p = jnp.exp(s - jnp.max(s, axis=-1, keepdims=True))
        l = jnp.sum(p, axis=-1, keepdims=True)
        o_ref[:, sl] = (jnp.dot(p.astype(BF16), vh, preferred_element_type=F32) / l).astype(o_ref.dtype)


def cross_attention(qm, mem_k, mem_v, heads, S, Bs, T, layer=None, prev_out=None):
    R, W = qm.shape
    dh = W // heads
    prompt = prev_out is None
    bm = _pick(S, 512, 16) if prompt else T
    off = 0 if prompt else S // bm
    steps = S // bm if prompt else Bs
    if prompt:
        M = mem_k.shape[0]
        kv_spec = pl.BlockSpec((M, W), lambda i: (0, 0))
    else:
        M = mem_k.shape[2]
        kv_spec = pl.BlockSpec((None, None, M, W), lambda i: (layer, i, 0, 0))
    in_specs = [pl.BlockSpec((bm, W), lambda i: (i + off, 0)), kv_spec, kv_spec]
    args = [qm, mem_k, mem_v]
    aliases = {}
    if not prompt:
        in_specs.append(pl.BlockSpec(memory_space=pl.ANY))
        args.append(prev_out)
        aliases[3] = 0
    return pl.pallas_call(
        functools.partial(_xattn_kernel, heads=heads, dh=dh, scale=dh ** -0.5, aliased=not prompt),
        grid=(steps,),
        in_specs=in_specs,
        out_specs=pl.BlockSpec((bm, W), lambda i: (i + off, 0)),
        out_shape=jax.ShapeDtypeStruct((R, W), BF16),
        input_output_aliases=aliases,
        compiler_params=_params("arbitrary"),
        name="xattn_prompt" if prompt else "xattn_sample",
    )(*args)


def _rope_tables(S, Bs, T, P, half):
    pos = jnp.concatenate([jnp.arange(S, dtype=jnp.int32),
                           jnp.tile(P + jnp.arange(T, dtype=jnp.int32), Bs)])
    inv_freq = ROPE_BASE ** (-jnp.arange(half, dtype=F32) / half)
    ang = pos.astype(F32)[:, None] * inv_freq[None, :]
    cos, sin = jnp.cos(ang), jnp.sin(ang)
    return jnp.concatenate([cos, cos, -sin, sin], axis=-1)


def _step(x_prompt, x_sample, cache_mla_ckv, cache_mla_krope, cache_mem_k, cache_mem_v,
          state_conv_a, state_conv_ffn, mem_prompt,
          norm_mix, w_in, q_norm, kv_norm, w_uq, w_ukv, conv_a, w_out_a, w_o_mla, w_o,
          norm_xattn, mem_norm, w_mq, w_mk, w_mv, w_mo, norm_ffn, w_up, conv_ffn, w_down,
          norm_final):
    Bp, S, D = x_prompt.shape
    Bs, T, _ = x_sample.shape
    DEPTH = w_in.shape[0]
    P = cache_mla_ckv.shape[2]
    C = cache_mla_ckv.shape[3]
    ROPE = cache_mla_krope.shape[3]
    half = ROPE // 2
    A = conv_a.shape[2]
    Q = q_norm.shape[1]
    H = w_ukv.shape[2]
    NOPE = w_uq.shape[2] // H - ROPE
    V = w_ukv.shape[3] - NOPE
    MEM = mem_prompt.shape[1]
    MH, MD = cache_mem_k.shape[3], cache_mem_k.shape[4]
    F = w_down.shape[1]
    assert Bp == 1 and NOPE == LANE and V == LANE and 4 * half == LANE
    R = S + Bs * T
    scale = (NOPE + ROPE) ** -0.5

    lay = {"A": A, "Q": Q, "C": C, "b": 0, "c": A, "h": 2 * A, "q": 3 * A, "kv": 3 * A + Q,
           "kr": 3 * A + Q + C}
    used = 3 * A + Q + C + LANE
    g0 = -(-used // 1024) * 1024
    lay["ga"], lay["gb"] = g0, g0 + D
    kr0 = 3 * A + Q + C

    x = jnp.concatenate([x_prompt.reshape(S, D), x_sample.reshape(Bs * T, D)], axis=0)
    tk = _rope_tables(S, Bs, T, P, half)
    qscale = scale * LOG2E
    tq = tk * qscale
    zeros_a = jnp.zeros((1, 2, A), F32)
    zeros_f = jnp.zeros((1, 2, 2 * F), F32)
    cache_krd = jnp.concatenate([cache_mla_krope, cache_mla_krope], axis=-1).astype(BF16)
    cmk = cache_mem_k.reshape(DEPTH, Bs, MEM, MH * MD)
    cmv = cache_mem_v.reshape(DEPTH, Bs, MEM, MH * MD)
    mem = mem_prompt.reshape(MEM, D)

    bf = lambda w: w.astype(BF16)
    w_in_t = relayout_w_in(jnp.swapaxes(w_in, 1, 2), norm_mix, kr0, half, g0, 2 * D)
    wq = w_uq.reshape(DEPTH, Q, H, NOPE + ROPE)
    x1, x2 = bf(wq[..., NOPE:NOPE + half]), bf(wq[..., NOPE + half:])
    w_uq_b = jnp.concatenate([bf(wq[..., :NOPE]), x1, x2, x2, x1], axis=-1).reshape(DEPTH, Q, H * 2 * LANE)
    w_uk_b = bf(w_ukv[..., :NOPE]).reshape(DEPTH, C, H * NOPE)
    w_uv_b = bf(w_ukv[..., NOPE:]).reshape(DEPTH, C, H * V)
    w_out_a_b, w_o_mla_b, w_o_b = bf(w_out_a), bf(w_o_mla), bf(w_o)
    w_mq_b = bf(norm_xattn[:, :, None] * w_mq)
    w_mk_b, w_mv_b, w_mo_b = bf(w_mk), bf(w_mv), bf(w_mo)
    w_up_b, w_down_b = bf(norm_ffn[:, :, None] * w_up), bf(w_down)

    outs = {k: [] for k in ("ckv_p", "kr_p", "mk", "mv", "ca_p", "cf_p", "ckv_s", "kr_s", "ca_s", "cf_s")}
    xb, rinv = row_stats(x)
    for l in range(DEPTH):
        proj = matmul_nt(xb, w_in_t, rinv, F32, l, name="matmul_in")
        po = post_in(proj, lay, zeros_a, conv_a[l], q_norm[l], kv_norm[l], tk, S, Bs, T)
        ya_in, cq, ckv, ckv_b, krc, krc_b, utail = post_in(
            proj, lay, state_conv_a[l], conv_a[l], q_norm[l], kv_norm[l], tk, S, Bs, T, prev_outs=po)
        qfull = matmul_q(cq, w_uq_b, l, tq, qscale)
        k_pad, v_ext = matmul_kv(ckv_b, krc_b, w_uk_b, w_uv_b, l, S)
        ob = prompt_attention(qfull, k_pad, v_ext, S, R, H)
        ob = sample_attention(qfull, w_uk_b, w_uv_b, cache_mla_ckv, cache_krd, l, ckv_b, krc_b, ob, S, Bs, T, H)
        merged = matmul_merge(ya_in, w_out_a_b, ob, w_o_mla_b, l, proj, lay["ga"], lay["gb"])
        x, xb, rinv = matmul(merged, w_o_b, F32, res=x, stats=True, layer=l, bn_target=512, name="matmul_o")

        mn = rmsnorm_rows(mem, mem_norm[l], BF16)
        mk = matmul(mn, w_mk_b, F32, layer=l, name="matmul_mk")
        mv = matmul(mn, w_mv_b, F32, layer=l, name="matmul_mv")
        qm = matmul(xb, w_mq_b, BF16, rinv=rinv, layer=l, name="matmul_mq")
        om = cross_attention(qm, mk, mv, MH, S, Bs, T)
        om = cross_attention(qm, cmk, cmv, MH, S, Bs, T, layer=l, prev_out=om)
        x, xb, rinv = matmul(om, w_mo_b, F32, res=x, stats=True, layer=l, bn_target=512, name="matmul_mo")

        up = matmul(xb, w_up_b, F32, rinv=rinv, layer=l, bn_target=512, name="matmul_up")
        hmid = ffn_gate(up, zeros_f, conv_ffn[l], S, Bs, T)
        hmid = ffn_gate(up, state_conv_ffn[l], conv_ffn[l], S, Bs, T, prev_out=hmid)
        if l + 1 < DEPTH:
            x, xb, rinv = matmul(hmid, w_down_b, F32, res=x, stats=True, layer=l, bn_target=512, k_splits=2,
                                 name="matmul_down")
        else:
            x = matmul(hmid, w_down_b, F32, res=x, layer=l, bn_target=512, k_splits=2, name="matmul_down")

        n_pt = S // _pick(S, EW_ROW_TILE, 16)
        outs["ckv_p"].append(ckv[:S].reshape(Bp, S, C))
        outs["kr_p"].append(krc[:S, :ROPE].reshape(Bp, S, ROPE))
        outs["mk"].append(mk.reshape(Bp, MEM, MH, MD))
        outs["mv"].append(mv.reshape(Bp, MEM, MH, MD))
        outs["ca_p"].append(utail[n_pt * HALO_ROWS - 2:n_pt * HALO_ROWS].reshape(Bp, 2, A))
        outs["cf_p"].append(up[S - 2:S].reshape(Bp, 2, 2 * F))
        outs["ckv_s"].append(ckv[S:].reshape(Bs, T, C))
        outs["kr_s"].append(krc[S:, :ROPE].reshape(Bs, T, ROPE))
        outs["ca_s"].append(utail[n_pt * HALO_ROWS:].reshape(Bs, HALO_ROWS, A)[:, HALO_ROWS - 2:])
        outs["cf_s"].append(up[S:].reshape(Bs, T, 2 * F)[:, T - 2:])

    y_prompt = rmsnorm_rows(x, norm_final, F32, row0=0, nrows=S).reshape(Bp, S, D)
    y_sample = rmsnorm_rows(x, norm_final, F32, row0=S, nrows=Bs * T).reshape(Bs, T, D)
    st = {k: jnp.stack(v) for k, v in outs.items()}
    return (y_prompt, y_sample, st["ckv_p"], st["kr_p"], st["mk"], st["mv"], st["ca_p"], st["cf_p"],
            st["ckv_s"], st["kr_s"], st["ca_s"], st["cf_s"])


def kernel(x_prompt, x_sample, cache_mla_ckv, cache_mla_krope, cache_mem_k, cache_mem_v, state_conv_a, state_conv_ffn, mem_prompt, norm_mix, w_in, q_norm, kv_norm, w_uq, w_ukv, conv_a, w_out_a, w_o_mla, w_o, norm_xattn, mem_norm, w_mq, w_mk, w_mv, w_mo, norm_ffn, w_up, conv_ffn, w_down, norm_final):
    return _step(x_prompt, x_sample, cache_mla_ckv, cache_mla_krope, cache_mem_k, cache_mem_v,
                 state_conv_a, state_conv_ffn, mem_prompt,
                 norm_mix, w_in, q_norm, kv_norm, w_uq, w_ukv, conv_a, w_out_a, w_o_mla, w_o,
                 norm_xattn, mem_norm, w_mq, w_mk, w_mv, w_mo, norm_ffn, w_up, conv_ffn, w_down,
                 norm_final)
```

```python
import functools

import jax
import jax.numpy as jnp
from jax import lax
from jax.experimental import pallas as pl
from jax.experimental.pallas import tpu as pltpu

F32 = jnp.float32
BF16 = jnp.bfloat16

EPS = 1e-6
CHUNK = 64
ROPE_BASE = 10000.0
NEG_BIG = -1e30

VMEM_LIMIT_BYTES = 60 * 1024 * 1024
LANE = 128
SUBLANE = 8
HALO_ROWS = 8
MM_ROW_TILE = 1040
EW_ROW_TILE = 256
FA_BLOCK_Q = 4096
FA_SUB_ROWS = 1024
FA_BLOCK_K = 512
FA_BLOCK_K_MAIN = 2048
LOG2E = 1.4426950408889634
NT_DIMS = (((1,), (1,)), ((), ()))


def _pick(n, target, mult):
    best = None
    for d in range(mult, min(n, target) + 1, mult):
        if n % d == 0:
            best = d
    return best or n


def _params(*sem):
    return pltpu.CompilerParams(dimension_semantics=sem, vmem_limit_bytes=VMEM_LIMIT_BYTES)


def _rms(x, g):
    return x * lax.rsqrt(jnp.mean(x * x, axis=-1, keepdims=True) + EPS) * g


def _rms_kernel(x_ref, g_ref, o_ref):
    o_ref[...] = _rms(x_ref[...].astype(F32), g_ref[...]).astype(o_ref.dtype)


def rmsnorm_rows(x, g, out_dtype, row0=0, nrows=None, bm_target=512):
    R, C = x.shape
    nrows = R - row0 if nrows is None else nrows
    bm = _pick(nrows, bm_target, 16)
    assert row0 % bm == 0
    off = row0 // bm
    return pl.pallas_call(
        _rms_kernel,
        grid=(nrows // bm,),
        in_specs=[pl.BlockSpec((bm, C), lambda i: (i + off, 0)),
                  pl.BlockSpec((1, C), lambda i: (0, 0))],
        out_specs=pl.BlockSpec((bm, C), lambda i: (i, 0)),
        out_shape=jax.ShapeDtypeStruct((nrows, C), out_dtype),
        compiler_params=_params("parallel"),
        name="rmsnorm",
    )(x, g.reshape(1, C).astype(F32))


def _scale_rows(acc, rinv):
    return jnp.concatenate([acc[:, LANE * c:LANE * (c + 1)] * rinv for c in range(acc.shape[1] // LANE)], axis=1)


def _mm_kernel(a_ref, w_ref, o_ref):
    o_ref[...] = jnp.dot(a_ref[...], w_ref[...], preferred_element_type=F32).astype(o_ref.dtype)


def _mm_rinv_kernel(a_ref, w_ref, rinv_ref, o_ref):
    acc = jnp.dot(a_ref[...], w_ref[...], preferred_element_type=F32)
    o_ref[...] = _scale_rows(acc, rinv_ref[...]).astype(o_ref.dtype)


def _mm_res_kernel(a_ref, w_ref, r_ref, o_ref):
    acc = jnp.dot(a_ref[...], w_ref[...], preferred_element_type=F32)
    o_ref[...] = (r_ref[...] + acc).astype(o_ref.dtype)


def _emit_row_stats(y, xb_ref, rinv_ref, j, nj):
    xb_ref[...] = y.astype(xb_ref.dtype)
    sq = y * y
    part = sq[:, 0:LANE]
    for c in range(1, y.shape[1] // LANE):
        part = part + sq[:, LANE * c:LANE * (c + 1)]

    @pl.when(j == 0)
    def _():
        rinv_ref[...] = part

    @pl.when(j > 0)
    def _():
        rinv_ref[...] += part

    @pl.when(j == nj - 1)
    def _():
        tot = jnp.sum(rinv_ref[...], axis=-1, keepdims=True)
        rinv_ref[...] = jnp.broadcast_to(lax.rsqrt(tot * (1.0 / (nj * y.shape[1])) + EPS), rinv_ref.shape)


def _mm_res_stats_kernel(a_ref, w_ref, r_ref, o_ref, xb_ref, rinv_ref, *, nj):
    y = r_ref[...] + jnp.dot(a_ref[...], w_ref[...], preferred_element_type=F32)
    o_ref[...] = y
    _emit_row_stats(y, xb_ref, rinv_ref, pl.program_id(1), nj)


def _row_stats_kernel(x_ref, xb_ref, rinv_ref):
    x = x_ref[...]
    xb_ref[...] = x.astype(xb_ref.dtype)
    rinv = lax.rsqrt(jnp.mean(x * x, axis=-1, keepdims=True) + EPS)
    rinv_ref[...] = jnp.broadcast_to(rinv, rinv_ref.shape)


def row_stats(x):
    R, C = x.shape
    bm = _pick(R, 512, 16)
    return pl.pallas_call(
        _row_stats_kernel,
        grid=(R // bm,),
        in_specs=[pl.BlockSpec((bm, C), lambda i: (i, 0))],
        out_specs=[pl.BlockSpec((bm, C), lambda i: (i, 0)), pl.BlockSpec((bm, LANE), lambda i: (i, 0))],
        out_shape=[jax.ShapeDtypeStruct((R, C), BF16), jax.ShapeDtypeStruct((R, LANE), F32)],
        compiler_params=_params("parallel"),
        name="row_stats",
    )(x)


def _wspec(w, layer, bn, kb=None, ks=0):
    kb = w.shape[-2] if kb is None else kb
    if w.ndim == 3:
        return pl.BlockSpec((None, kb, bn), lambda i, j: (layer, ks, j))
    return pl.BlockSpec((kb, bn), lambda i, j: (ks, j))


def matmul(a, w, out_dtype, res=None, rinv=None, stats=False, layer=None, bm_target=MM_ROW_TILE,
           bn_target=1024, k_splits=1, name="matmul"):
    if k_splits > 1:
        assert res is not None and rinv is None and a.shape[1] % (k_splits * LANE) == 0
        for ks in range(k_splits):
            last = ks == k_splits - 1
            res = _matmul_call(a, w, out_dtype, res, None, stats and last, layer, bm_target, bn_target,
                               a.shape[1] // k_splits, ks, name)
        return res
    return _matmul_call(a, w, out_dtype, res, rinv, stats, layer, bm_target, bn_target, a.shape[1], 0, name)


def _matmul_call(a, w, out_dtype, res, rinv, stats, layer, bm_target, bn_target, K, ks, name):
    M = a.shape[0]
    N = w.shape[-1]
    bm = _pick(M, bm_target, 16)
    bn = _pick(N, bn_target, LANE)
    in_specs = [pl.BlockSpec((bm, K), lambda i, j: (i, ks)), _wspec(w, layer, bn, K, ks)]
    args = [a, w]
    body = _mm_kernel
    out_specs = pl.BlockSpec((bm, bn), lambda i, j: (i, j))
    out_shape = jax.ShapeDtypeStruct((M, N), out_dtype)
    sem = ("parallel", "parallel")
    if res is not None:
        assert rinv is None
        in_specs.append(pl.BlockSpec((bm, bn), lambda i, j: (i, j)))
        args.append(res)
        body = _mm_res_kernel
        if stats:
            assert out_dtype == F32
            body = functools.partial(_mm_res_stats_kernel, nj=N // bn)
            out_specs = [out_specs, pl.BlockSpec((bm, bn), lambda i, j: (i, j)),
                         pl.BlockSpec((bm, LANE), lambda i, j: (i, 0))]
            out_shape = [out_shape, jax.ShapeDtypeStruct((M, N), BF16), jax.ShapeDtypeStruct((M, LANE), F32)]
            sem = ("parallel", "arbitrary")
    elif rinv is not None:
        in_specs.append(pl.BlockSpec((bm, LANE), lambda i, j: (i, 0)))
        args.append(rinv)
        body = _mm_rinv_kernel
    return pl.pallas_call(
        body,
        grid=(M // bm, N // bn),
        in_specs=in_specs,
        out_specs=out_specs,
        out_shape=out_shape,
        compiler_params=_params(*sem),
        name=name,
    )(*args)


def _mm_nt_rinv_kernel(a_ref, wt_ref, rinv_ref, o_ref):
    acc = lax.dot_general(a_ref[...], wt_ref[...], NT_DIMS, preferred_element_type=F32)
    o_ref[...] = _scale_rows(acc, rinv_ref[...]).astype(o_ref.dtype)


def matmul_nt(a, wt, rinv, out_dtype, layer, bn_target=1024, name="matmul_nt"):
    M, K = a.shape
    N = wt.shape[1]
    bm = _pick(M, MM_ROW_TILE, 16)
    bn = _pick(N, bn_target, LANE)
    return pl.pallas_call(
        _mm_nt_rinv_kernel,
        grid=(M // bm, N // bn),
        in_specs=[pl.BlockSpec((bm, K), lambda i, j: (i, 0)),
                  pl.BlockSpec((None, bn, K), lambda i, j: (layer, j, 0)),
                  pl.BlockSpec((bm, LANE), lambda i, j: (i, 0))],
        out_specs=pl.BlockSpec((bm, bn), lambda i, j: (i, j)),
        out_shape=jax.ShapeDtypeStruct((M, N), out_dtype),
        compiler_params=_params("parallel", "parallel"),
        name=name,
    )(a, wt, rinv)


def _win_kernel(w_ref, g_ref, o_ref, *, kr0, half, g0):
    n_gate = o_ref.shape[0] - g0
    dt = o_ref.dtype
    g = g_ref[...]
    o_ref[0:kr0, :] = (w_ref[0:kr0, :] * g).astype(dt)
    x1 = (w_ref[kr0:kr0 + half, :] * g).astype(dt)
    x2 = (w_ref[kr0 + half:kr0 + 2 * half, :] * g).astype(dt)
    o_ref[kr0:kr0 + half, :] = x1
    o_ref[kr0 + half:kr0 + 2 * half, :] = x2
    o_ref[kr0 + 2 * half:kr0 + 3 * half, :] = x2
    o_ref[kr0 + 3 * half:kr0 + 4 * half, :] = x1
    o_ref[kr0 + 4 * half:g0, :] = jnp.zeros((g0 - kr0 - 4 * half, o_ref.shape[1]), dt)
    o_ref[g0:, :] = (w_ref[kr0 + 2 * half:kr0 + 2 * half + n_gate, :] * g).astype(dt)


def relayout_w_in(w_in_t, gain, kr0, half, g0, n_gate):
    DEPTH, NC, K = w_in_t.shape
    assert kr0 % 16 == 0 and half % 16 == 0 and g0 % 16 == 0 and kr0 + 2 * half + n_gate == NC
    return pl.pallas_call(
        functools.partial(_win_kernel, kr0=kr0, half=half, g0=g0),
        grid=(DEPTH, K // LANE),
        in_specs=[pl.BlockSpec((None, NC, LANE), lambda l, i: (l, 0, i)),
                  pl.BlockSpec((None, 1, LANE), lambda l, i: (l, 0, i))],
        out_specs=pl.BlockSpec((None, g0 + n_gate, LANE), lambda l, i: (l, 0, i)),
        out_shape=jax.ShapeDtypeStruct((DEPTH, g0 + n_gate, K), BF16),
        compiler_params=_params("parallel", "parallel"),
        name="relayout_w_in",
    )(w_in_t, gain.reshape(DEPTH, 1, K))


def _mm_q_kernel(a_ref, w_ref, t_ref, o_ref, *, heads, scale):
    acc = jnp.dot(a_ref[...], w_ref[...], preferred_element_type=F32)
    t = t_ref[...]
    for h in range(heads):
        c0 = 2 * LANE * h
        o_ref[:, c0:c0 + LANE] = (acc[:, c0:c0 + LANE] * scale).astype(o_ref.dtype)
        o_ref[:, c0 + LANE:c0 + 2 * LANE] = (acc[:, c0 + LANE:c0 + 2 * LANE] * t).astype(o_ref.dtype)


def matmul_q(a, w, layer, tq, scale, heads_per_tile=4):
    M, K = a.shape
    N = w.shape[-1]
    bm = _pick(M, MM_ROW_TILE, 16)
    bn = 2 * LANE * heads_per_tile
    assert N % bn == 0
    return pl.pallas_call(
        functools.partial(_mm_q_kernel, heads=heads_per_tile, scale=scale),
        grid=(M // bm, N // bn),
        in_specs=[pl.BlockSpec((bm, K), lambda i, j: (i, 0)),
                  _wspec(w, layer, bn),
                  pl.BlockSpec((bm, LANE), lambda i, j: (i, 0))],
        out_specs=pl.BlockSpec((bm, bn), lambda i, j: (i, j)),
        out_shape=jax.ShapeDtypeStruct((M, N), BF16),
        compiler_params=_params("parallel", "parallel"),
        name="matmul_q",
    )(a, w, tq)


def _mm_kv_kernel(a_ref, wk_ref, wv_ref, kr_ref, k_ref, v_ref, *, heads):
    a = a_ref[...]
    kn = jnp.dot(a, wk_ref[...], preferred_element_type=F32)
    vn = jnp.dot(a, wv_ref[...], preferred_element_type=F32)
    kr = kr_ref[...]
    ones = jnp.ones(kr.shape, v_ref.dtype)
    for h in range(heads):
        lo, mid, hi = 2 * LANE * h, 2 * LANE * h + LANE, 2 * LANE * (h + 1)
        k_ref[:, lo:mid] = kn[:, LANE * h:LANE * (h + 1)].astype(k_ref.dtype)
        k_ref[:, mid:hi] = kr
        v_ref[:, lo:mid] = vn[:, LANE * h:LANE * (h + 1)].astype(v_ref.dtype)
        v_ref[:, mid:hi] = ones


def matmul_kv(ckv_b, krc_b, w_uk, w_uv, layer, S, heads_per_tile=4):
    C = ckv_b.shape[1]
    N = w_uk.shape[-1]
    bm = _pick(S, 1024, 16)
    bn = LANE * heads_per_tile
    assert N % bn == 0
    return pl.pallas_call(
        functools.partial(_mm_kv_kernel, heads=heads_per_tile),
        grid=(S // bm, N // bn),
        in_specs=[pl.BlockSpec((bm, C), lambda i, j: (i, 0)),
                  _wspec(w_uk, layer, bn),
                  _wspec(w_uv, layer, bn),
                  pl.BlockSpec((bm, LANE), lambda i, j: (i, 0))],
        out_specs=[pl.BlockSpec((bm, 2 * bn), lambda i, j: (i, j)),
                   pl.BlockSpec((bm, 2 * bn), lambda i, j: (i, j))],
        out_shape=[jax.ShapeDtypeStruct((S, 2 * N), BF16),
                   jax.ShapeDtypeStruct((S, 2 * N), BF16)],
        compiler_params=_params("parallel", "parallel"),
        name="matmul_kv",
    )(ckv_b, w_uk, w_uv, krc_b)


def _sigmoid(x):
    return 1.0 / (1.0 + jnp.exp(-x))


def _mm_merge_kernel(a1_ref, w1_ref, a2_ref, w2_ref, ga_ref, gb_ref, o_ref):
    ya = jnp.dot(a1_ref[...], w1_ref[...], preferred_element_type=F32)
    yb = jnp.dot(a2_ref[...], w2_ref[...], preferred_element_type=F32)
    o_ref[...] = (_sigmoid(ga_ref[...]) * ya + _sigmoid(gb_ref[...]) * yb).astype(o_ref.dtype)


def matmul_merge(ya_in, w_out_a, ob, w_o_mla, layer, proj, ga_col0, gb_col0):
    M, K1 = ya_in.shape
    K2 = ob.shape[1]
    N = w_out_a.shape[-1]
    bm = _pick(M, MM_ROW_TILE, 16)
    bn = _pick(N, 512, LANE)
    assert ga_col0 % bn == 0 and gb_col0 % bn == 0
    ja, jb = ga_col0 // bn, gb_col0 // bn
    return pl.pallas_call(
        _mm_merge_kernel,
        grid=(M // bm, N // bn),
        in_specs=[pl.BlockSpec((bm, K1), lambda i, j: (i, 0)),
                  _wspec(w_out_a, layer, bn),
                  pl.BlockSpec((bm, K2), lambda i, j: (i, 0)),
                  _wspec(w_o_mla, layer, bn),
                  pl.BlockSpec((bm, bn), lambda i, j: (i, j + ja)),
                  pl.BlockSpec((bm, bn), lambda i, j: (i, j + jb))],
        out_specs=pl.BlockSpec((bm, bn), lambda i, j: (i, j)),
        out_shape=jax.ShapeDtypeStruct((M, N), BF16),
        compiler_params=_params("parallel", "parallel"),
        name="matmul_merge",
    )(ya_in, w_out_a, ob, w_o_mla, proj, proj)


def _conv3(u, prev8, w):
    rows, C = u.shape
    u3 = jnp.concatenate([prev8, u], axis=0).reshape(rows // SUBLANE + 1, SUBLANE, C)
    sub = lax.broadcasted_iota(jnp.int32, (1, SUBLANE, 1), 1)
    r1 = pltpu.roll(u3, 1, 1)
    r2 = pltpu.roll(u3, 2, 1)
    p1 = jnp.where(sub < 1, r1[:-1], r1[1:])
    p2 = jnp.where(sub < 2, r2[:-1], r2[1:])
    y = w[0:1][None] * p2 + w[1:2][None] * p1 + w[2:3][None] * u3[1:]
    return y.reshape(rows, C)


def _halo(prev_u, st_ref, first):
    st = st_ref[...]
    sub = lax.broadcasted_iota(jnp.int32, (HALO_ROWS, 1), 0)
    st8 = jnp.where(sub == HALO_ROWS - 2, st[0:1], jnp.where(sub == HALO_ROWS - 1, st[1:2], 0.0))
    if prev_u is None:
        return st8
    return jnp.where(first, st8, prev_u)


def _postin_kernel(*refs, has_prev):
    if has_prev:
        (b_ref, c_ref, h_ref, cp_ref, hp_ref, ql_ref, kv_ref, kr_ref, st_ref, taps_ref, qn_ref, kvn_ref,
         t_ref, ya_ref, cq_ref, ckv_ref, ckvb_ref, krc_ref, krcb_ref, ut_ref) = refs
        prev_u = cp_ref[...] * hp_ref[...]
    else:
        (b_ref, c_ref, h_ref, ql_ref, kv_ref, kr_ref, st_ref, taps_ref, qn_ref, kvn_ref,
         t_ref, ya_ref, cq_ref, ckv_ref, ckvb_ref, krc_ref, krcb_ref, ut_ref) = refs
        prev_u = None
    u = c_ref[...] * h_ref[...]
    prev8 = _halo(prev_u, st_ref, pl.program_id(0) == 0)
    ya_ref[...] = (b_ref[...] * _conv3(u, prev8, taps_ref[...])).astype(ya_ref.dtype)
    ut_ref[...] = u[u.shape[0] - HALO_ROWS:, :]
    cq_ref[...] = _rms(ql_ref[...], qn_ref[...]).astype(cq_ref.dtype)
    ckv = _rms(kv_ref[...], kvn_ref[...])
    ckv_ref[...] = ckv
    ckvb_ref[...] = ckv.astype(ckvb_ref.dtype)
    t = kr_ref[...] * t_ref[...]
    krc = t + pltpu.roll(t, LANE // 2, 1)
    krc_ref[...] = krc
    krcb_ref[...] = krc.astype(krcb_ref.dtype)


def post_in(proj, lay, state, taps, q_norm, kv_norm, tk, S, Bs, T, prev_outs=None):
    R = proj.shape[0]
    A, Q, C = lay["A"], lay["Q"], lay["C"]
    prompt = prev_outs is None
    bm = _pick(S, EW_ROW_TILE, 16) if prompt else T
    assert bm % 16 == 0 and S % bm == 0
    off = 0 if prompt else S // bm
    steps = S // bm if prompt else Bs
    rb = bm // HALO_ROWS

    def col(width, col0):
        assert col0 % width == 0
        return pl.BlockSpec((bm, width), lambda i, c=col0 // width: (i + off, c))

    in_specs = [col(A, lay["b"]), col(A, lay["c"]), col(A, lay["h"])]
    args = [proj, proj, proj]
    if prompt:
        for name in ("c", "h"):
            in_specs.append(pl.BlockSpec((HALO_ROWS, A),
                                         lambda i, c=lay[name] // A: (jnp.maximum(i * rb - 1, 0), c)))
            args.append(proj)
    in_specs += [col(Q, lay["q"]), col(C, lay["kv"]), col(LANE, lay["kr"]),
                 pl.BlockSpec((None, 2, A), (lambda i: (0, 0, 0)) if prompt else (lambda i: (i, 0, 0))),
                 pl.BlockSpec((3, A), lambda i: (0, 0)),
                 pl.BlockSpec((1, Q), lambda i: (0, 0)),
                 pl.BlockSpec((1, C), lambda i: (0, 0)),
                 pl.BlockSpec((bm, LANE), lambda i: (i + off, 0))]
    args += [proj, proj, proj, state, taps, q_norm.reshape(1, Q), kv_norm.reshape(1, C), tk]
    out_widths = [(A, BF16), (Q, BF16), (C, F32), (C, BF16), (LANE, F32), (LANE, BF16)]
    out_specs = [pl.BlockSpec((bm, w), lambda i: (i + off, 0)) for w, _ in out_widths]
    out_shape = [jax.ShapeDtypeStruct((R, w), dt) for w, dt in out_widths]
    n_pt = S // _pick(S, EW_ROW_TILE, 16)
    toff = 0 if prompt else n_pt
    out_specs.append(pl.BlockSpec((HALO_ROWS, A), lambda i: (i + toff, 0)))
    out_shape.append(jax.ShapeDtypeStruct(((n_pt + Bs) * HALO_ROWS, A), F32))
    aliases = {}
    if not prompt:
        n_in = len(args)
        for k, arr in enumerate(prev_outs):
            in_specs.append(pl.BlockSpec(memory_space=pl.ANY))
            args.append(arr)
            aliases[n_in + k] = k
    return pl.pallas_call(
        functools.partial(_postin_kernel_aliased if not prompt else _postin_kernel, has_prev=prompt),
        grid=(steps,),
        in_specs=in_specs,
        out_specs=out_specs,
        out_shape=out_shape,
        input_output_aliases=aliases,
        compiler_params=_params("arbitrary"),
        name="post_in_prompt" if prompt else "post_in_sample",
    )(*args)


def _postin_kernel_aliased(*refs, has_prev):
    n_out = 7
    n_in = len(refs) - 2 * n_out
    _postin_kernel(*refs[:n_in], *refs[n_in + n_out:], has_prev=has_prev)


def _ffn_kernel(*refs, has_prev, aliased):
    if aliased:
        refs = refs[:-2] + refs[-1:]
    if has_prev:
        a_ref, g_ref, ap_ref, gp_ref, sa_ref, sg_ref, wa_ref, wg_ref, o_ref = refs
    else:
        a_ref, g_ref, sa_ref, sg_ref, wa_ref, wg_ref, o_ref = refs
    first = pl.program_id(0) == 0
    sub =lax.broadcasted_iota(jnp.int32, (HALO_ROWS, 1), 0)

    def column(c, carry):
        cs = pl.ds(pl.multiple_of(c * LANE, LANE), LANE)

        def conv(x_ref, p_ref, s_ref, w_ref):
            st = s_ref[:, cs]
            prev8 = jnp.where(sub == HALO_ROWS - 2, st[0:1], jnp.where(sub == HALO_ROWS - 1, st[1:2], 0.0))
            if p_ref is not None:
                prev8 = jnp.where(first, prev8, p_ref[:, cs])
            return _conv3(x_ref[:, cs], prev8, w_ref[:, cs])

        a = conv(a_ref, ap_ref if has_prev else None, sa_ref, wa_ref)
        g = conv(g_ref, gp_ref if has_prev else None, sg_ref, wg_ref)
        o_ref[:, cs] = (g * a / (1.0 + jnp.exp2(g * (-LOG2E)))).astype(o_ref.dtype)
        return carry

    lax.fori_loop(0, o_ref.shape[1] // LANE, column, 0)


def ffn_gate(up, state, taps, S, Bs, T, prev_out=None):
    R, F2 = up.shape
    F = F2 // 2
    prompt = prev_out is None
    bm = _pick(S, EW_ROW_TILE, 16) if prompt else T
    bc = _pick(F, 5504, LANE)
    nc = F // bc
    off = 0 if prompt else S // bm
    steps = S // bm if prompt else Bs
    rb = bm // HALO_ROWS
    in_specs = [pl.BlockSpec((bm, bc), lambda i, j: (i + off, j)),
                pl.BlockSpec((bm, bc), lambda i, j: (i + off, j + nc))]
    args = [up, up]
    if prompt:
        in_specs += [pl.BlockSpec((HALO_ROWS, bc), lambda i, j: (jnp.maximum(i * rb - 1, 0), j)),
                     pl.BlockSpec((HALO_ROWS, bc), lambda i, j: (jnp.maximum(i * rb - 1, 0), j + nc))]
        args += [up, up]
    bsel = (lambda i: 0) if prompt else (lambda i: i)
    in_specs += [pl.BlockSpec((None, 2, bc), lambda i, j: (bsel(i), 0, j)),
                 pl.BlockSpec((None, 2, bc), lambda i, j: (bsel(i), 0, j + nc)),
                 pl.BlockSpec((3, bc), lambda i, j: (0, j)),
                 pl.BlockSpec((3, bc), lambda i, j: (0, j + nc))]
    args += [state, state, taps, taps]
    aliases = {}
    if not prompt:
        in_specs.append(pl.BlockSpec(memory_space=pl.ANY))
        args.append(prev_out)
        aliases[len(args) - 1] = 0
    return pl.pallas_call(
        functools.partial(_ffn_kernel, has_prev=prompt, aliased=not prompt),
        grid=(steps, nc),
        in_specs=in_specs,
        out_specs=pl.BlockSpec((bm, bc), lambda i, j: (i + off, j)),
        out_shape=jax.ShapeDtypeStruct((R, F), BF16),
        input_output_aliases=aliases,
        compiler_params=_params("arbitrary", "arbitrary"),
        name="ffn_gate_prompt" if prompt else "ffn_gate_sample",
    )(*args)


def _fa_kernel(q_ref, k_ref, v_ref, o_ref, m_sc, acc_sc, *, bq, bk, bk_main, sub):
    i = pl.program_id(1)
    m_sc[...] = jnp.full(m_sc.shape, NEG_BIG, F32)
    acc_sc[...] = jnp.zeros(acc_sc.shape, F32)
    chains = bq // sub

    def chain_block(row0, nrows, k0, bk, local_k0):
        lane_tiles = bk // LANE
        rows = slice(row0, row0 + nrows)
        kj = k_ref[pl.ds(k0, bk), :]
        vj = v_ref[pl.ds(k0, bk), :]
        s = lax.dot_general(q_ref[rows, :], kj, NT_DIMS, preferred_element_type=F32)
        if local_k0 is not None:
            kchunk = (lax.broadcasted_iota(jnp.int32, (nrows, bk), 1) + local_k0) // CHUNK
            qchunk = (lax.broadcasted_iota(jnp.int32, (nrows, bk), 0) + row0) // CHUNK
            s = jnp.where(kchunk <= qchunk, s, NEG_BIG)
        cols = [s[:, LANE * c:LANE * (c + 1)] for c in range(lane_tiles)]
        m_cur = cols[0]
        for c in cols[1:]:
            m_cur = jnp.maximum(m_cur, c)
        m_prev = m_sc[rows, :]
        m_new = jnp.maximum(m_prev, jnp.max(m_cur, axis=-1, keepdims=True))
        alpha = jnp.exp2(m_prev - m_new)
        p = jnp.concatenate([jnp.exp2(c - m_new) for c in cols], axis=1).astype(vj.dtype)
        pv = jnp.dot(p, vj, preferred_element_type=F32)
        acc_sc[rows, :] = jnp.concatenate([alpha, alpha], axis=1) * acc_sc[rows, :] + pv
        m_sc[rows, :] = m_new

    def body(j, carry):
        k0 = pl.multiple_of(j * bk_main, bk_main)
        for b in range(bk_main // bk):
            for r in range(chains):
                chain_block(sub * r, sub, k0 + b * bk, bk, None)
        return carry

    lax.fori_loop(0, i * (bq // bk_main), body, 0)
    t0 = pl.multiple_of(i * bq, bq)
    for b in range(bq // bk):
        for r in range(chains):
            lo = max(sub * r, b * bk)
            if lo < sub * (r + 1):
                chain_block(lo, sub * (r + 1) - lo, t0 + b * bk, bk,
                            None if (b + 1) * bk <= sub * r else b * bk)
    acc = acc_sc[...]
    o_ref[...] = (acc[:, :LANE] / acc[:, LANE:]).astype(o_ref.dtype)


def prompt_attention(q_pad, k_pad, v_ext, S, R, H):
    bq = _pick(S, FA_BLOCK_Q, CHUNK)
    sub = _pick(bq, FA_SUB_ROWS, CHUNK)
    bk = _pick(sub, FA_BLOCK_K, LANE)
    bk_main = _pick(bq, FA_BLOCK_K_MAIN, bk)
    assert bq % bk_main == 0 and sub % bk == 0 and S % bq == 0
    return pl.pallas_call(
        functools.partial(_fa_kernel, bq=bq, bk=bk, bk_main=bk_main, sub=sub),
        grid=(H, S // bq),
        in_specs=[pl.BlockSpec((bq, 2 * LANE), lambda h, i: (i, h)),
                  pl.BlockSpec((S, 2 * LANE), lambda h, i: (0, h)),
                  pl.BlockSpec((S, 2 * LANE), lambda h, i: (0, h))],
        out_specs=pl.BlockSpec((bq, LANE), lambda h, i: (i, h)),
        out_shape=jax.ShapeDtypeStruct((R, H * LANE), BF16),
        scratch_shapes=[pltpu.VMEM((bq, LANE), F32), pltpu.VMEM((bq, 2 * LANE), F32)],
        compiler_params=_params("parallel", "arbitrary"),
        name="prompt_attention",
    )(q_pad, k_pad, v_ext)


def _sattn_kernel(q_ref, wuk_ref, wuv_ref, cckv_ref, ckr_ref, nckv_ref, nkr_ref, ob_ref, o_ref,
                  ql_sc, ka_sc, *, H, T, C, P, KPAD):
    del ob_ref
    for h in range(H):
        qn = q_ref[:, 2 * LANE * h:2 * LANE * h + LANE]
        qlat = lax.dot_general(qn, wuk_ref[:, LANE * h:LANE * (h + 1)], NT_DIMS, preferred_element_type=F32)
        ql_sc[T * h:T * (h + 1), 0:C] = qlat.astype(ql_sc.dtype)
        ql_sc[T * h:T * (h + 1), C:C + LANE] = q_ref[:, 2 * LANE * h + LANE:2 * LANE * (h + 1)]
    ka_sc[0:P, 0:C] = cckv_ref[...].astype(ka_sc.dtype)
    ka_sc[0:P, C:C + LANE] = ckr_ref[...]
    ka_sc[P:P + T, 0:C] = nckv_ref[...]
    ka_sc[P:P + T, C:C + LANE] = nkr_ref[...]
    ka_sc[P + T:P + KPAD, :] = jnp.zeros((KPAD - T, C + LANE), ka_sc.dtype)
    ka = ka_sc[...]
    s = lax.dot_general(ql_sc[...], ka, NT_DIMS, preferred_element_type=F32)
    kpos = lax.broadcasted_iota(jnp.int32, s.shape, 1)
    qpos = P + lax.broadcasted_iota(jnp.int32, s.shape, 0) % T
    s = jnp.where((kpos < P + T) & (kpos // CHUNK <= qpos // CHUNK), s, NEG_BIG)
    m = jnp.max(s, axis=-1, keepdims=True)
    p = jnp.exp2(s - m)
    l = jnp.sum(p, axis=-1, keepdims=True)
    olat = (jnp.dot(p.astype(ka.dtype), ka[:, 0:C], preferred_element_type=F32) / l).astype(wuv_ref.dtype)
    for h in range(H):
        oh = jnp.dot(olat[T * h:T * (h + 1), :], wuv_ref[:, LANE * h:LANE * (h + 1)], preferred_element_type=F32)
        o_ref[:, LANE * h:LANE * (h + 1)] = oh.astype(o_ref.dtype)


def sample_attention(qfull, w_uk, w_uv, cache_ckv, cache_krd, layer, ckv_b, krc_b, ob, S, Bs, T, H):
    C = w_uk.shape[-2]
    P = cache_ckv.shape[2]
    KPAD = LANE
    assert T % 16 == 0 and S % T == 0 and T <= KPAD and w_uk.ndim == 3
    off = S // T
    return pl.pallas_call(
        functools.partial(_sattn_kernel, H=H, T=T, C=C, P=P, KPAD=KPAD),
        grid=(Bs,),
        in_specs=[pl.BlockSpec((T, 2 * LANE * H), lambda b: (b + off, 0)),
                  pl.BlockSpec((None, C, LANE * H), lambda b: (layer, 0, 0)),
                  pl.BlockSpec((None, C, LANE * H), lambda b: (layer, 0, 0)),
                  pl.BlockSpec((None, None, P, C), lambda b: (layer, b, 0, 0)),
                  pl.BlockSpec((None, None, P, LANE), lambda b: (layer, b, 0, 0)),
                  pl.BlockSpec((T, C), lambda b: (b + off, 0)),
                  pl.BlockSpec((T, LANE), lambda b: (b + off, 0)),
                  pl.BlockSpec(memory_space=pl.ANY)],
        out_specs=pl.BlockSpec((T, LANE * H), lambda b: (b + off, 0)),
        out_shape=jax.ShapeDtypeStruct(ob.shape, ob.dtype),
        scratch_shapes=[pltpu.VMEM((H * T, C + LANE), BF16), pltpu.VMEM((P + KPAD, C + LANE), BF16)],
        input_output_aliases={7: 0},
        compiler_params=_params("arbitrary"),
        name="sample_attention",
    )(qfull, w_uk, w_uv, cache_ckv, cache_krd, ckv_b, krc_b, ob)


def _xattn_kernel(*refs, heads, dh, scale, aliased):
    if aliased:
        q_ref, k_ref, v_ref, _, o_ref = refs
    else:
        q_ref, k_ref, v_ref, o_ref = refs
    for h in range(heads):
        sl = slice(dh * h, dh * (h + 1))
        kh = k_ref[:, sl].astype(BF16)
        vh = v_ref[:, sl].astype(BF16)
        s = lax.dot_general(q_ref[:, sl], kh, NT_DIMS, preferred_element_type=F32) * scale
        p = jnp.exp(s - jnp.max(s, axis=-1, keepdims=True))
        l = jnp.sum(p, axis=-1, keepdims=True)
        o_ref[:, sl] = (jnp.dot(p.astype(BF16), vh, preferred_element_type=F32) / l).astype(o_ref.dtype)


def cross_attention(qm, mem_k, mem_v, heads, S, Bs, T, layer=None, prev_out=None):
    R, W = qm.shape
    dh = W // heads
    prompt = prev_out is None
    bm = _pick(S, 512, 16) if prompt else T
    off = 0 if prompt else S // bm
    steps = S // bm if prompt else Bs
    if prompt:
        M = mem_k.shape[0]
        kv_spec = pl.BlockSpec((M, W), lambda i: (0, 0))
    else:
        M = mem_k.shape[2]
        kv_spec = pl.BlockSpec((None, None, M, W), lambda i: (layer, i, 0, 0))
    in_specs = [pl.BlockSpec((bm, W), lambda i: (i + off, 0)), kv_spec, kv_spec]
    args = [qm, mem_k, mem_v]
    aliases = {}
    if not prompt:
        in_specs.append(pl.BlockSpec(memory_space=pl.ANY))
        args.append(prev_out)
        aliases[3] = 0
    return pl.pallas_call(
        functools.partial(_xattn_kernel, heads=heads, dh=dh, scale=dh ** -0.5, aliased=not prompt),
        grid=(steps,),
        in_specs=in_specs,
        out_specs=pl.BlockSpec((bm, W), lambda i: (i + off, 0)),
        out_shape=jax.ShapeDtypeStruct((R, W), BF16),
        input_output_aliases=aliases,
        compiler_params=_params("arbitrary"),
        name="xattn_prompt" if prompt else "xattn_sample",
    )(*args)


def _rope_tables(S, Bs, T, P, half):
    pos = jnp.concatenate([jnp.arange(S, dtype=jnp.int32),
                           jnp.tile(P + jnp.arange(T, dtype=jnp.int32), Bs)])
    inv_freq = ROPE_BASE ** (-jnp.arange(half, dtype=F32) / half)
    ang = pos.astype(F32)[:, None] * inv_freq[None, :]
    cos, sin = jnp.cos(ang), jnp.sin(ang)
    return jnp.concatenate([cos, cos, -sin, sin], axis=-1)


def _step(x_prompt, x_sample, cache_mla_ckv, cache_mla_krope, cache_mem_k, cache_mem_v,
          state_conv_a, state_conv_ffn, mem_prompt,
          norm_mix, w_in, q_norm, kv_norm, w_uq, w_ukv, conv_a, w_out_a, w_o_mla, w_o,
          norm_xattn, mem_norm, w_mq, w_mk, w_mv, w_mo, norm_ffn, w_up, conv_ffn, w_down,
          norm_final):
    Bp, S, D = x_prompt.shape
    Bs, T, _ = x_sample.shape
    DEPTH = w_in.shape[0]
    P = cache_mla_ckv.shape[2]
    C = cache_mla_ckv.shape[3]
    ROPE = cache_mla_krope.shape[3]
    half = ROPE // 2
    A = conv_a.shape[2]
    Q = q_norm.shape[1]
    H = w_ukv.shape[2]
    NOPE = w_uq.shape[2] // H - ROPE
    V = w_ukv.shape[3] - NOPE
    MEM = mem_prompt.shape[1]
    MH, MD = cache_mem_k.shape[3], cache_mem_k.shape[4]
    F = w_down.shape[1]
    assert Bp == 1 and NOPE == LANE and V == LANE and 4 * half == LANE
    R = S + Bs * T
    scale = (NOPE + ROPE) ** -0.5

    lay = {"A": A, "Q": Q, "C": C, "b": 0, "c": A, "h": 2 * A, "q": 3 * A, "kv": 3 * A + Q,
           "kr": 3 * A + Q + C}
    used = 3 * A + Q + C + LANE
    g0 = -(-used // 1024) * 1024
    lay["ga"], lay["gb"] = g0, g0 + D
    kr0 = 3 * A + Q + C

    x = jnp.concatenate([x_prompt.reshape(S, D), x_sample.reshape(Bs * T, D)], axis=0)
    tk = _rope_tables(S, Bs, T, P, half)
    qscale = scale * LOG2E
    tq = tk * qscale
    zeros_a = jnp.zeros((1, 2, A), F32)
    zeros_f = jnp.zeros((1, 2, 2 * F), F32)
    cache_krd = jnp.concatenate([cache_mla_krope, cache_mla_krope], axis=-1).astype(BF16)
    cmk = cache_mem_k.reshape(DEPTH, Bs, MEM, MH * MD)
    cmv = cache_mem_v.reshape(DEPTH, Bs, MEM, MH * MD)
    mem = mem_prompt.reshape(MEM, D)

    bf = lambda w: w.astype(BF16)
    w_in_t = relayout_w_in(jnp.swapaxes(w_in, 1, 2), norm_mix, kr0, half, g0, 2 * D)
    wq = w_uq.reshape(DEPTH, Q, H, NOPE + ROPE)
    x1, x2 = bf(wq[..., NOPE:NOPE + half]), bf(wq[..., NOPE + half:])
    w_uq_b = jnp.concatenate([bf(wq[..., :NOPE]), x1, x2, x2, x1], axis=-1).reshape(DEPTH, Q, H * 2 * LANE)
    w_uk_b = bf(w_ukv[..., :NOPE]).reshape(DEPTH, C, H * NOPE)
    w_uv_b = bf(w_ukv[..., NOPE:]).reshape(DEPTH, C, H * V)
    w_out_a_b, w_o_mla_b, w_o_b = bf(w_out_a), bf(w_o_mla), bf(w_o)
    w_mq_b = bf(norm_xattn[:, :, None] * w_mq)
    w_mk_b, w_mv_b, w_mo_b = bf(w_mk), bf(w_mv), bf(w_mo)
    w_up_b, w_down_b = bf(norm_ffn[:, :, None] * w_up), bf(w_down)

    outs = {k: [] for k in ("ckv_p", "kr_p", "mk", "mv", "ca_p", "cf_p", "ckv_s", "kr_s", "ca_s", "cf_s")}
    xb, rinv = row_stats(x)
    for l in range(DEPTH):
        proj = matmul_nt(xb, w_in_t, rinv, F32, l, name="matmul_in")
        po = post_in(proj, lay, zeros_a, conv_a[l], q_norm[l], kv_norm[l], tk, S, Bs, T)
        ya_in, cq, ckv, ckv_b, krc, krc_b, utail = post_in(
            proj, lay, state_conv_a[l], conv_a[l], q_norm[l], kv_norm[l], tk, S, Bs, T, prev_outs=po)
        qfull = matmul_q(cq, w_uq_b, l, tq, qscale)
        k_pad, v_ext = matmul_kv(ckv_b, krc_b, w_uk_b, w_uv_b, l, S)
        ob = prompt_attention(qfull, k_pad, v_ext, S, R, H)
        ob = sample_attention(qfull, w_uk_b, w_uv_b, cache_mla_ckv, cache_krd, l, ckv_b, krc_b, ob, S, Bs, T, H)
        merged = matmul_merge(ya_in, w_out_a_b, ob, w_o_mla_b, l, proj, lay["ga"], lay["gb"])
        x, xb, rinv = matmul(merged, w_o_b, F32, res=x, stats=True, layer=l, bn_target=512, name="matmul_o")

        mn = rmsnorm_rows(mem, mem_norm[l], BF16)
        mk = matmul(mn, w_mk_b, F32, layer=l, name="matmul_mk")
        mv = matmul(mn, w_mv_b, F32, layer=l, name="matmul_mv")
        qm = matmul(xb, w_mq_b, BF16, rinv=rinv, layer=l, name="matmul_mq")
        om = cross_attention(qm, mk, mv, MH, S, Bs, T)
        om = cross_attention(qm, cmk, cmv, MH, S, Bs, T, layer=l, prev_out=om)
        x, xb, rinv = matmul(om, w_mo_b, F32, res=x, stats=True, layer=l, bn_target=512, name="matmul_mo")

        up = matmul(xb, w_up_b, F32, rinv=rinv, layer=l, bn_target=512, name="matmul_up")
        hmid = ffn_gate(up, zeros_f, conv_ffn[l], S, Bs, T)
        hmid = ffn_gate(up, state_conv_ffn[l], conv_ffn[l], S, Bs, T, prev_out=hmid)
        if l + 1 < DEPTH:
            x, xb, rinv = matmul(hmid, w_down_b, F32, res=x, stats=True, layer=l, bn_target=512, k_splits=2,
                                 name="matmul_down")
        else:
            x = matmul(hmid, w_down_b, F32, res=x, layer=l, bn_target=512, k_splits=2, name="matmul_down")

        n_pt = S // _pick(S, EW_ROW_TILE, 16)
        outs["ckv_p"].append(ckv[:S].reshape(Bp, S, C))
        outs["kr_p"].append(krc[:S, :ROPE].reshape(Bp, S, ROPE))
        outs["mk"].append(mk.reshape(Bp, MEM, MH, MD))
        outs["mv"].append(mv.reshape(Bp, MEM, MH, MD))
        outs["ca_p"].append(utail[n_pt * HALO_ROWS - 2:n_pt * HALO_ROWS].reshape(Bp, 2, A))
        outs["cf_p"].append(up[S - 2:S].reshape(Bp, 2, 2 * F))
        outs["ckv_s"].append(ckv[S:].reshape(Bs, T, C))
        outs["kr_s"].append(krc[S:, :ROPE].reshape(Bs, T, ROPE))
        outs["ca_s"].append(utail[n_pt * HALO_ROWS:].reshape(Bs, HALO_ROWS, A)[:, HALO_ROWS - 2:])
        outs["cf_s"].append(up[S:].reshape(Bs, T, 2 * F)[:, T - 2:])

    y_prompt = rmsnorm_rows(x, norm_final, F32, row0=0, nrows=S).reshape(Bp, S, D)
    y_sample = rmsnorm_rows(x, norm_final, F32, row0=S, nrows=Bs * T).reshape(Bs, T, D)
    st = {k: jnp.stack(v) for k, v in outs.items()}
    return (y_prompt, y_sample, st["ckv_p"], st["kr_p"], st["mk"], st["mv"], st["ca_p"], st["cf_p"],
            st["ckv_s"], st["kr_s"], st["ca_s"], st["cf_s"])


def kernel(x_prompt, x_sample, cache_mla_ckv, cache_mla_krope, cache_mem_k, cache_mem_v, state_conv_a, state_conv_ffn, mem_prompt, norm_mix, w_in, q_norm, kv_norm, w_uq, w_ukv, conv_a, w_out_a, w_o_mla, w_o, norm_xattn, mem_norm, w_mq, w_mk, w_mv, w_mo, norm_ffn, w_up, conv_ffn, w_down, norm_final):
    return _step(x_prompt, x_sample, cache_mla_ckv, cache_mla_krope, cache_mem_k, cache_mem_v,
                 state_conv_a, state_conv_ffn, mem_prompt,
                 norm_mix, w_in, q_norm, kv_norm, w_uq, w_ukv, conv_a, w_out_a, w_o_mla, w_o,
                 norm_xattn, mem_norm, w_mq, w_mk, w_mv, w_mo, norm_ffn, w_up, conv_ffn, w_down,
                 norm_final)
```

```python
import functools

import jax
import jax.numpy as jnp
from jax import lax
from jax.experimental import pallas as pl
from jax.experimental.pallas import tpu as pltpu

F32 = jnp.float32
BF16 = jnp.bfloat16

EPS = 1e-6
CHUNK = 64
ROPE_BASE = 10000.0
NEG_BIG = -1e30

VMEM_LIMIT_BYTES = 60 * 1024 * 1024
LANE = 128
SUBLANE = 8
HALO_ROWS = 8
MM_ROW_TILE = 1040
EW_ROW_TILE = 256
FA_BLOCK_Q = 4096
FA_SUB_ROWS = 1024
FA_BLOCK_K = 512
FA_BLOCK_K_MAIN = 2048
LOG2E = 1.4426950408889634
NT_DIMS = (((1,), (1,)), ((), ()))


def _pick(n, target, mult):
    best = None
    for d in range(mult, min(n, target) + 1, mult):
        if n % d == 0:
            best = d
    return best or n


def _params(*sem):
    return pltpu.CompilerParams(dimension_semantics=sem, vmem_limit_bytes=VMEM_LIMIT_BYTES)


def _rms(x, g):
    return x * lax.rsqrt(jnp.mean(x * x, axis=-1, keepdims=True) + EPS) * g


def _rms_kernel(x_ref, g_ref, o_ref):
    o_ref[...] = _rms(x_ref[...].astype(F32), g_ref[...]).astype(o_ref.dtype)


def rmsnorm_rows(x, g, out_dtype, row0=0, nrows=None, bm_target=512):
    R, C = x.shape
    nrows = R - row0 if nrows is None else nrows
    bm = _pick(nrows, bm_target, 16)
    assert row0 % bm == 0
    off = row0 // bm
    return pl.pallas_call(
        _rms_kernel,
        grid=(nrows // bm,),
        in_specs=[pl.BlockSpec((bm, C), lambda i: (i + off, 0)),
                  pl.BlockSpec((1, C), lambda i: (0, 0))],
        out_specs=pl.BlockSpec((bm, C), lambda i: (i, 0)),
        out_shape=jax.ShapeDtypeStruct((nrows, C), out_dtype),
        compiler_params=_params("parallel"),
        name="rmsnorm",
    )(x, g.reshape(1, C).astype(F32))


def _scale_rows(acc, rinv):
    return jnp.concatenate([acc[:, LANE * c:LANE * (c + 1)] * rinv for c in range(acc.shape[1] // LANE)], axis=1)


def _mm_kernel(a_ref, w_ref, o_ref):
    o_ref[...] = jnp.dot(a_ref[...], w_ref[...], preferred_element_type=F32).astype(o_ref.dtype)


def _mm_rinv_kernel(a_ref, w_ref, rinv_ref, o_ref):
    acc = jnp.dot(a_ref[...], w_ref[...], preferred_element_type=F32)
    o_ref[...] = _scale_rows(acc, rinv_ref[...]).astype(o_ref.dtype)


def _mm_res_kernel(a_ref, w_ref, r_ref, o_ref):
    acc = jnp.dot(a_ref[...], w_ref[...], preferred_element_type=F32)
    o_ref[...] = (r_ref[...] + acc).astype(o_ref.dtype)


def _emit_row_stats(y, xb_ref, rinv_ref, j, nj):
    xb_ref[...] = y.astype(xb_ref.dtype)
    sq = y * y
    part = sq[:, 0:LANE]
    for c in range(1, y.shape[1] // LANE):
        part = part + sq[:, LANE * c:LANE * (c + 1)]

    @pl.when(j == 0)
    def _():
        rinv_ref[...] = part

    @pl.when(j > 0)
    def _():
        rinv_ref[...] += part

    @pl.when(j == nj - 1)
    def _():
        tot = jnp.sum(rinv_ref[...], axis=-1, keepdims=True)
        rinv_ref[...] = jnp.broadcast_to(lax.rsqrt(tot * (1.0 / (nj * y.shape[1])) + EPS), rinv_ref.shape)


def _mm_res_stats_kernel(a_ref, w_ref, r_ref, o_ref, xb_ref, rinv_ref, *, nj):
    y = r_ref[...] + jnp.dot(a_ref[...], w_ref[...], preferred_element_type=F32)
    o_ref[...] = y
    _emit_row_stats(y, xb_ref, rinv_ref, pl.program_id(1), nj)


def _row_stats_kernel(x_ref, xb_ref, rinv_ref):
    x = x_ref[...]
    xb_ref[...] = x.astype(xb_ref.dtype)
    rinv = lax.rsqrt(jnp.mean(x * x, axis=-1, keepdims=True) + EPS)
    rinv_ref[...] = jnp.broadcast_to(rinv, rinv_ref.shape)


def row_stats(x):
    R, C = x.shape
    bm = _pick(R, 512, 16)
    return pl.pallas_call(
        _row_stats_kernel,
        grid=(R // bm,),
        in_specs=[pl.BlockSpec((bm, C), lambda i: (i, 0))],
        out_specs=[pl.BlockSpec((bm, C), lambda i: (i, 0)), pl.BlockSpec((bm, LANE), lambda i: (i, 0))],
        out_shape=[jax.ShapeDtypeStruct((R, C), BF16), jax.ShapeDtypeStruct((R, LANE), F32)],
        compiler_params=_params("parallel"),
        name="row_stats",
    )(x)


def _wspec(w, layer, bn, kb=None, ks=0):
    kb = w.shape[-2] if kb is None else kb
    if w.ndim == 3:
        return pl.BlockSpec((None, kb, bn), lambda i, j: (layer, ks, j))
    return pl.BlockSpec((kb, bn), lambda i, j: (ks, j))


def matmul(a, w, out_dtype, res=None, rinv=None, stats=False, layer=None, bm_target=MM_ROW_TILE,
           bn_target=1024, k_splits=1, name="matmul"):
    if k_splits > 1:
        assert res is not None and rinv is None and a.shape[1] % (k_splits * LANE) == 0
        for ks in range(k_splits):
            last = ks == k_splits - 1
            res = _matmul_call(a, w, out_dtype, res, None, stats and last, layer, bm_target, bn_target,
                               a.shape[1] // k_splits, ks, name)
        return res
    return _matmul_call(a, w, out_dtype, res, rinv, stats, layer, bm_target, bn_target, a.shape[1], 0, name)


def _matmul_call(a, w, out_dtype, res, rinv, stats, layer, bm_target, bn_target, K, ks, name):
    M = a.shape[0]
    N = w.shape[-1]
    bm = _pick(M, bm_target, 16)
    bn = _pick(N, bn_target, LANE)
    in_specs = [pl.BlockSpec((bm, K), lambda i, j: (i, ks)), _wspec(w, layer, bn, K, ks)]
    args = [a, w]
    body = _mm_kernel
    out_specs = pl.BlockSpec((bm, bn), lambda i, j: (i, j))
    out_shape = jax.ShapeDtypeStruct((M, N), out_dtype)
    sem = ("parallel", "parallel")
    if res is not None:
        assert rinv is None
        in_specs.append(pl.BlockSpec((bm, bn), lambda i, j: (i, j)))
        args.append(res)
        body = _mm_res_kernel
        if stats:
            assert out_dtype == F32
            body = functools.partial(_mm_res_stats_kernel, nj=N // bn)
            out_specs = [out_specs, pl.BlockSpec((bm, bn), lambda i, j: (i, j)),
                         pl.BlockSpec((bm, LANE), lambda i, j: (i, 0))]
            out_shape = [out_shape, jax.ShapeDtypeStruct((M, N), BF16), jax.ShapeDtypeStruct((M, LANE), F32)]
            sem = ("parallel", "arbitrary")
    elif rinv is not None:
        in_specs.append(pl.BlockSpec((bm, LANE), lambda i, j: (i, 0)))
        args.append(rinv)
        body = _mm_rinv_kernel
    return pl.pallas_call(
        body,
        grid=(M // bm, N // bn),
        in_specs=in_specs,
        out_specs=out_specs,
        out_shape=out_shape,
        compiler_params=_params(*sem),
        name=name,
    )(*args)


def _mm_nt_rinv_kernel(a_ref, wt_ref, rinv_ref, o_ref):
    acc = lax.dot_general(a_ref[...], wt_ref[...], NT_DIMS, preferred_element_type=F32)
    o_ref[...] = _scale_rows(acc, rinv_ref[...]).astype(o_ref.dtype)


def matmul_nt(a, wt, rinv, out_dtype, layer, bn_target=1024, name="matmul_nt"):
    M, K = a.shape
    N = wt.shape[1]
    bm = _pick(M, MM_ROW_TILE, 16)
    bn = _pick(N, bn_target, LANE)
    return pl.pallas_call(
        _mm_nt_rinv_kernel,
        grid=(M // bm, N // bn),
        in_specs=[pl.BlockSpec((bm, K), lambda i, j: (i, 0)),
                  pl.BlockSpec((None, bn, K), lambda i, j: (layer, j, 0)),
                  pl.BlockSpec((bm, LANE), lambda i, j: (i, 0))],
        out_specs=pl.BlockSpec((bm, bn), lambda i, j: (i, j)),
        out_shape=jax.ShapeDtypeStruct((M, N), out_dtype),
        compiler_params=_params("parallel", "parallel"),
        name=name,
    )(a, wt, rinv)


def _win_kernel(w_ref, g_ref, o_ref, *, kr0, half, g0):
    n_gate = o_ref.shape[0] - g0
    dt = o_ref.dtype
    g = g_ref[...]
    o_ref[0:kr0, :] = (w_ref[0:kr0, :] * g).astype(dt)
    x1 = (w_ref[kr0:kr0 + half, :] * g).astype(dt)
    x2 = (w_ref[kr0 + half:kr0 + 2 * half, :] * g).astype(dt)
    o_ref[kr0:kr0 + half, :] = x1
    o_ref[kr0 + half:kr0 + 2 * half, :] = x2
    o_ref[kr0 + 2 * half:kr0 + 3 * half, :] = x2
    o_ref[kr0 + 3 * half:kr0 + 4 * half, :] = x1
    o_ref[kr0 + 4 * half:g0, :] = jnp.zeros((g0 - kr0 - 4 * half, o_ref.shape[1]), dt)
    o_ref[g0:, :] = (w_ref[kr0 + 2 * half:kr0 + 2 * half + n_gate, :] * g).astype(dt)


def relayout_w_in(w_in_t, gain, kr0, half, g0, n_gate):
    DEPTH, NC, K = w_in_t.shape
    assert kr0 % 16 == 0 and half % 16 == 0 and g0 % 16 == 0 and kr0 + 2 * half + n_gate == NC
    return pl.pallas_call(
        functools.partial(_win_kernel, kr0=kr0, half=half, g0=g0),
        grid=(DEPTH, K // LANE),
        in_specs=[pl.BlockSpec((None, NC, LANE), lambda l, i: (l, 0, i)),
                  pl.BlockSpec((None, 1, LANE), lambda l, i: (l, 0, i))],
        out_specs=pl.BlockSpec((None, g0 + n_gate, LANE), lambda l, i: (l, 0, i)),
        out_shape=jax.ShapeDtypeStruct((DEPTH, g0 + n_gate, K), BF16),
        compiler_params=_params("parallel", "parallel"),
        name="relayout_w_in",
    )(w_in_t, gain.reshape(DEPTH, 1, K))


def _mm_q_kernel(a_ref, w_ref, t_ref, o_ref, *, heads, scale):
    acc = jnp.dot(a_ref[...], w_ref[...], preferred_element_type=F32)
    t = t_ref[...]
    for h in range(heads):
        c0 = 2 * LANE * h
        o_ref[:, c0:c0 + LANE] = (acc[:, c0:c0 + LANE] * scale).astype(o_ref.dtype)
        o_ref[:, c0 + LANE:c0 + 2 * LANE] = (acc[:, c0 + LANE:c0 + 2 * LANE] * t).astype(o_ref.dtype)


def matmul_q(a, w, layer, tq, scale, heads_per_tile=4):
    M, K = a.shape
    N = w.shape[-1]
    bm = _pick(M, MM_ROW_TILE, 16)
    bn = 2 * LANE * heads_per_tile
    assert N % bn == 0
    return pl.pallas_call(
        functools.partial(_mm_q_kernel, heads=heads_per_tile, scale=scale),
        grid=(M // bm, N // bn),
        in_specs=[pl.BlockSpec((bm, K), lambda i, j: (i, 0)),
                  _wspec(w, layer, bn),
                  pl.BlockSpec((bm, LANE), lambda i, j: (i, 0))],
        out_specs=pl.BlockSpec((bm, bn), lambda i, j: (i, j)),
        out_shape=jax.ShapeDtypeStruct((M, N), BF16),
        compiler_params=_params("parallel", "parallel"),
        name="matmul_q",
    )(a, w, tq)


def _mm_kv_kernel(a_ref, wk_ref, wv_ref, kr_ref, k_ref, v_ref, *, heads):
    a = a_ref[...]
    kn = jnp.dot(a, wk_ref[...], preferred_element_type=F32)
    vn = jnp.dot(a, wv_ref[...], preferred_element_type=F32)
    kr = kr_ref[...]
    ones = jnp.ones(kr.shape, v_ref.dtype)
    for h in range(heads):
        lo, mid, hi = 2 * LANE * h, 2 * LANE * h + LANE, 2 * LANE * (h + 1)
        k_ref[:, lo:mid] = kn[:, LANE * h:LANE * (h + 1)].astype(k_ref.dtype)
        k_ref[:, mid:hi] = kr
        v_ref[:, lo:mid] = vn[:, LANE * h:LANE * (h + 1)].astype(v_ref.dtype)
        v_ref[:, mid:hi] = ones


def matmul_kv(ckv_b, krc_b, w_uk, w_uv, layer, S, heads_per_tile=4):
    C = ckv_b.shape[1]
    N = w_uk.shape[-1]
    bm = _pick(S, 1024, 16)
    bn = LANE * heads_per_tile
    assert N % bn == 0
    return pl.pallas_call(
        functools.partial(_mm_kv_kernel, heads=heads_per_tile),
        grid=(S // bm, N // bn),
        in_specs=[pl.BlockSpec((bm, C), lambda i, j: (i, 0)),
                  _wspec(w_uk, layer, bn),
                  _wspec(w_uv, layer, bn),
                  pl.BlockSpec((bm, LANE), lambda i, j: (i, 0))],
        out_specs=[pl.BlockSpec((bm, 2 * bn), lambda i, j: (i, j)),
                   pl.BlockSpec((bm, 2 * bn), lambda i, j: (i, j))],
        out_shape=[jax.ShapeDtypeStruct((S, 2 * N), BF16),
                   jax.ShapeDtypeStruct((S, 2 * N), BF16)],
        compiler_params=_params("parallel", "parallel"),
        name="matmul_kv",
    )(ckv_b, w_uk, w_uv, krc_b)


def _sigmoid(x):
    return 1.0 / (1.0 + jnp.exp(-x))


def _mm_merge_kernel(a1_ref, w1_ref, a2_ref, w2_ref, ga_ref, gb_ref, o_ref):
    ya = jnp.dot(a1_ref[...], w1_ref[...], preferred_element_type=F32)
    yb = jnp.dot(a2_ref[...], w2_ref[...], preferred_element_type=F32)
    o_ref[...] = (_sigmoid(ga_ref[...]) * ya + _sigmoid(gb_ref[...]) * yb).astype(o_ref.dtype)


def matmul_merge(ya_in, w_out_a, ob, w_o_mla, layer, proj, ga_col0, gb_col0):
    M, K1 = ya_in.shape
    K2 = ob.shape[1]
    N = w_out_a.shape[-1]
    bm = _pick(M, MM_ROW_TILE, 16)
    bn = _pick(N, 512, LANE)
    assert ga_col0 % bn == 0 and gb_col0 % bn == 0
    ja, jb = ga_col0 // bn, gb_col0 // bn
    return pl.pallas_call(
        _mm_merge_kernel,
        grid=(M // bm, N // bn),
        in_specs=[pl.BlockSpec((bm, K1), lambda i, j: (i, 0)),
                  _wspec(w_out_a, layer, bn),
                  pl.BlockSpec((bm, K2), lambda i, j: (i, 0)),
                  _wspec(w_o_mla, layer, bn),
                  pl.BlockSpec((bm, bn), lambda i, j: (i, j + ja)),
                  pl.BlockSpec((bm, bn), lambda i, j: (i, j + jb))],
        out_specs=pl.BlockSpec((bm, bn), lambda i, j: (i, j)),
        out_shape=jax.ShapeDtypeStruct((M, N), BF16),
        compiler_params=_params("parallel", "parallel"),
        name="matmul_merge",
    )(ya_in, w_out_a, ob, w_o_mla, proj, proj)


def _conv3(u, prev8, w, x_sc):
    rows = u.shape[0]
    x_sc[0:HALO_ROWS, :] = prev8
    x_sc[HALO_ROWS:, :] = u
    return (w[0:1] * x_sc[HALO_ROWS - 2:HALO_ROWS - 2 + rows, :]
            + w[1:2] * x_sc[HALO_ROWS - 1:HALO_ROWS - 1 + rows, :] + w[2:3] * u)


def _halo(prev_u, st, first):
    sub = lax.broadcasted_iota(jnp.int32, (HALO_ROWS, 1), 0)
    st8 = jnp.where(sub == HALO_ROWS - 2, st[0:1], jnp.where(sub == HALO_ROWS - 1, st[1:2], 0.0))
    if prev_u is None:
        return st8
    return jnp.where(first, st8, prev_u)


def _postin_kernel(*refs, has_prev):
    x_sc = refs[-1]
    refs = refs[:-1]
    if has_prev:
        (b_ref, c_ref, h_ref, cp_ref, hp_ref, ql_ref, kv_ref, kr_ref, st_ref, taps_ref, qn_ref, kvn_ref,
         t_ref, ya_ref, cq_ref, ckv_ref, ckvb_ref, krc_ref, krcb_ref, ut_ref) = refs
    else:
        (b_ref, c_ref, h_ref, ql_ref, kv_ref, kr_ref, st_ref, taps_ref, qn_ref, kvn_ref,
         t_ref, ya_ref, cq_ref, ckv_ref, ckvb_ref, krc_ref, krcb_ref, ut_ref) = refs
    first = pl.program_id(0) == 0
    rows = ya_ref.shape[0]

    def column(c, carry):
        cs = pl.ds(pl.multiple_of(c * LANE, LANE), LANE)
        u = c_ref[:, cs] * h_ref[:, cs]
        prev8 = _halo(cp_ref[:, cs] * hp_ref[:, cs] if has_prev else None, st_ref[:, cs], first)
        ya_ref[:, cs] = (b_ref[:, cs] * _conv3(u, prev8, taps_ref[:, cs], x_sc)).astype(ya_ref.dtype)
        ut_ref[:, cs] = u[rows - HALO_ROWS:, :]
        return carry

    lax.fori_loop(0, ya_ref.shape[1] // LANE, column, 0, unroll=2)
    cq_ref[...] = _rms(ql_ref[...], qn_ref[...]).astype(cq_ref.dtype)
    ckv = _rms(kv_ref[...], kvn_ref[...])
    ckv_ref[...] = ckv
    ckvb_ref[...] = ckv.astype(ckvb_ref.dtype)
    t = kr_ref[...] * t_ref[...]
    krc = t + pltpu.roll(t, LANE // 2, 1)
    krc_ref[...] = krc
    krcb_ref[...] = krc.astype(krcb_ref.dtype)


def post_in(proj, lay, state, taps, q_norm, kv_norm, tk, S, Bs, T, prev_outs=None):
    R = proj.shape[0]
    A, Q, C = lay["A"], lay["Q"], lay["C"]
    prompt = prev_outs is None
    bm = _pick(S, EW_ROW_TILE, 16) if prompt else T
    assert bm % 16 == 0 and S % bm == 0
    off = 0 if prompt else S // bm
    steps = S // bm if prompt else Bs
    rb = bm // HALO_ROWS

    def col(width, col0):
        assert col0 % width == 0
        return pl.BlockSpec((bm, width), lambda i, c=col0 // width: (i + off, c))

    in_specs = [col(A, lay["b"]), col(A, lay["c"]), col(A, lay["h"])]
    args = [proj, proj, proj]
    if prompt:
        for name in ("c", "h"):
            in_specs.append(pl.BlockSpec((HALO_ROWS, A),
                                         lambda i, c=lay[name] // A: (jnp.maximum(i * rb - 1, 0), c)))
            args.append(proj)
    in_specs += [col(Q, lay["q"]), col(C, lay["kv"]), col(LANE, lay["kr"]),
                 pl.BlockSpec((None, 2, A), (lambda i: (0, 0, 0)) if prompt else (lambda i: (i, 0, 0))),
                 pl.BlockSpec((3, A), lambda i: (0, 0)),
                 pl.BlockSpec((1, Q), lambda i: (0, 0)),
                 pl.BlockSpec((1, C), lambda i: (0, 0)),
                 pl.BlockSpec((bm, LANE), lambda i: (i + off, 0))]
    args += [proj, proj, proj, state, taps, q_norm.reshape(1, Q), kv_norm.reshape(1, C), tk]
    out_widths = [(A, BF16), (Q, BF16), (C, F32), (C, BF16), (LANE, F32), (LANE, BF16)]
    out_specs = [pl.BlockSpec((bm, w), lambda i: (i + off, 0)) for w, _ in out_widths]
    out_shape = [jax.ShapeDtypeStruct((R, w), dt) for w, dt in out_widths]
    n_pt = S // _pick(S, EW_ROW_TILE, 16)
    toff = 0 if prompt else n_pt
    out_specs.append(pl.BlockSpec((HALO_ROWS, A), lambda i: (i + toff, 0)))
    out_shape.append(jax.ShapeDtypeStruct(((n_pt + Bs) * HALO_ROWS, A), F32))
    aliases = {}
    if not prompt:
        n_in = len(args)
        for k, arr in enumerate(prev_outs):
            in_specs.append(pl.BlockSpec(memory_space=pl.ANY))
            args.append(arr)
            aliases[n_in + k] = k
    return pl.pallas_call(
        functools.partial(_postin_kernel_aliased if not prompt else _postin_kernel, has_prev=prompt),
        grid=(steps,),
        in_specs=in_specs,
        out_specs=out_specs,
        out_shape=out_shape,
        scratch_shapes=[pltpu.VMEM((bm + HALO_ROWS, LANE), F32)],
        input_output_aliases=aliases,
        compiler_params=_params("arbitrary"),
        name="post_in_prompt" if prompt else "post_in_sample",
    )(*args)


def _postin_kernel_aliased(*refs, has_prev):
    n_out = 7
    n_in = len(refs) - 2 * n_out - 1
    _postin_kernel(*refs[:n_in], *refs[n_in + n_out:], has_prev=has_prev)


def _ffn_kernel(*refs, has_prev, aliased):
    xa_sc, xg_sc = refs[-2:]
    refs = refs[:-2]
    if aliased:
        refs = refs[:-2] + refs[-1:]
    if has_prev:
        a_ref, g_ref, ap_ref, gp_ref, sa_ref, sg_ref, wa_ref, wg_ref, o_ref = refs
    else:
        a_ref, g_ref, sa_ref, sg_ref, wa_ref, wg_ref, o_ref = refs
    first = pl.program_id(0) == 0

    def column(c, carry):
        cs = pl.ds(pl.multiple_of(c * LANE, LANE), LANE)

        def conv(x_ref, p_ref, s_ref, w_ref, x_sc):
            prev8 = _halo(None if p_ref is None else p_ref[:, cs], s_ref[:, cs], first)
            return _conv3(x_ref[:, cs], prev8, w_ref[:, cs], x_sc)

        a = conv(a_ref, ap_ref if has_prev else None, sa_ref, wa_ref, xa_sc)
        g = conv(g_ref, gp_ref if has_prev else None, sg_ref, wg_ref, xg_sc)
        o_ref[:, cs] = (g * a / (1.0 + jnp.exp2(g * (-LOG2E)))).astype(o_ref.dtype)
        return carry

    lax.fori_loop(0, o_ref.shape[1] // LANE, column, 0, unroll=2)


def ffn_gate(up, state, taps, S, Bs, T, prev_out=None):
    R, F2 = up.shape
    F = F2 // 2
    prompt = prev_out is None
    bm = _pick(S, EW_ROW_TILE, 16) if prompt else T
    bc = _pick(F, 5504, LANE)
    nc = F // bc
    off = 0 if prompt else S // bm
    steps = S // bm if prompt else Bs
    rb = bm // HALO_ROWS
    in_specs = [pl.BlockSpec((bm, bc), lambda i, j: (i + off, j)),
                pl.BlockSpec((bm, bc), lambda i, j: (i + off, j + nc))]
    args = [up, up]
    if prompt:
        in_specs += [pl.BlockSpec((HALO_ROWS, bc), lambda i, j: (jnp.maximum(i * rb - 1, 0), j)),
                     pl.BlockSpec((HALO_ROWS, bc), lambda i, j: (jnp.maximum(i * rb - 1, 0), j + nc))]
        args += [up, up]
    bsel = (lambda i: 0) if prompt else (lambda i: i)
    in_specs += [pl.BlockSpec((None, 2, bc), lambda i, j: (bsel(i), 0, j)),
                 pl.BlockSpec((None, 2, bc), lambda i, j: (bsel(i), 0, j + nc)),
                 pl.BlockSpec((3, bc), lambda i, j: (0, j)),
                 pl.BlockSpec((3, bc), lambda i, j: (0, j + nc))]
    args += [state, state, taps, taps]
    aliases = {}
    if not prompt:
        in_specs.append(pl.BlockSpec(memory_space=pl.ANY))
        args.append(prev_out)
        aliases[len(args) - 1] = 0
    return pl.pallas_call(
        functools.partial(_ffn_kernel, has_prev=prompt, aliased=not prompt),
        grid=(steps, nc),
        in_specs=in_specs,
        out_specs=pl.BlockSpec((bm, bc), lambda i, j: (i + off, j)),
        out_shape=jax.ShapeDtypeStruct((R, F), BF16),
        scratch_shapes=[pltpu.VMEM((bm + HALO_ROWS, LANE), F32), pltpu.VMEM((bm + HALO_ROWS, LANE), F32)],
        input_output_aliases=aliases,
        compiler_params=_params("arbitrary", "arbitrary"),
        name="ffn_gate_prompt" if prompt else "ffn_gate_sample",
    )(*args)


def _fa_kernel(q_ref, k_ref, v_ref, o_ref, m_sc, acc_sc, *, bq, bk, bk_main, sub):
    i = pl.program_id(1)
    m_sc[...] = jnp.full(m_sc.shape, NEG_BIG, F32)
    acc_sc[...] = jnp.zeros(acc_sc.shape, F32)
    chains = bq // sub

    def chain_block(row0, nrows, k0, bk, local_k0):
        lane_tiles = bk // LANE
        rows = slice(row0, row0 + nrows)
        kj = k_ref[pl.ds(k0, bk), :]
        vj = v_ref[pl.ds(k0, bk), :]
        s = lax.dot_general(q_ref[rows, :], kj, NT_DIMS, preferred_element_type=F32)
        if local_k0 is not None:
            kchunk = (lax.broadcasted_iota(jnp.int32, (nrows, bk), 1) + local_k0) // CHUNK
            qchunk = (lax.broadcasted_iota(jnp.int32, (nrows, bk), 0) + row0) // CHUNK
            s = jnp.where(kchunk <= qchunk, s, NEG_BIG)
        cols = [s[:, LANE * c:LANE * (c + 1)] for c in range(lane_tiles)]
        m_cur = cols[0]
        for c in cols[1:]:
            m_cur = jnp.maximum(m_cur, c)
        m_prev = m_sc[rows, :]
        m_new = jnp.maximum(m_prev, jnp.max(m_cur, axis=-1, keepdims=True))
        alpha = jnp.exp2(m_prev - m_new)
        p = jnp.concatenate([jnp.exp2(c - m_new) for c in cols], axis=1).astype(vj.dtype)
        pv = jnp.dot(p, vj, preferred_element_type=F32)
        acc_sc[rows, :] = jnp.concatenate([alpha, alpha], axis=1) * acc_sc[rows, :] + pv
        m_sc[rows, :] = m_new

    def body(j, carry):
        k0 = pl.multiple_of(j * bk_main, bk_main)
        for b in range(bk_main // bk):
            for r in range(chains):
                chain_block(sub * r, sub, k0 + b * bk, bk, None)
        return carry

    lax.fori_loop(0, i * (bq // bk_main), body, 0)
    t0 = pl.multiple_of(i * bq, bq)
    for b in range(bq // bk):
        for r in range(chains):
            lo = max(sub * r, b * bk)
            if lo < sub * (r + 1):
                chain_block(lo, sub * (r + 1) - lo, t0 + b * bk, bk,
                            None if (b + 1) * bk <= sub * r else b * bk)
    acc = acc_sc[...]
    o_ref[...] = (acc[:, :LANE] / acc[:, LANE:]).astype(o_ref.dtype)


def prompt_attention(q_pad, k_pad, v_ext, S, R, H):
    bq = _pick(S, FA_BLOCK_Q, CHUNK)
    sub = _pick(bq, FA_SUB_ROWS, CHUNK)
    bk = _pick(sub, FA_BLOCK_K, LANE)
    bk_main = _pick(bq, FA_BLOCK_K_MAIN, bk)
    assert bq % bk_main == 0 and sub % bk == 0 and S % bq == 0
    return pl.pallas_call(
        functools.partial(_fa_kernel, bq=bq, bk=bk, bk_main=bk_main, sub=sub),
        grid=(H, S // bq),
        in_specs=[pl.BlockSpec((bq, 2 * LANE), lambda h, i: (i, h)),
                  pl.BlockSpec((S, 2 * LANE), lambda h, i: (0, h)),
                  pl.BlockSpec((S, 2 * LANE), lambda h, i: (0, h))],
        out_specs=pl.BlockSpec((bq, LANE), lambda h, i: (i, h)),
        out_shape=jax.ShapeDtypeStruct((R, H * LANE), BF16),
        scratch_shapes=[pltpu.VMEM((bq, LANE), F32), pltpu.VMEM((bq, 2 * LANE), F32)],
        compiler_params=_params("parallel", "arbitrary"),
        name="prompt_attention",
    )(q_pad, k_pad, v_ext)


def _sattn_kernel(q_ref, wuk_ref, wuv_ref, cckv_ref, ckr_ref, nckv_ref, nkr_ref, ob_ref, o_ref,
                  ql_sc, ka_sc, *, H, T, C, P, KPAD):
    del ob_ref
    for h in range(H):
        qn = q_ref[:, 2 * LANE * h:2 * LANE * h + LANE]
        qlat = lax.dot_general(qn, wuk_ref[:, LANE * h:LANE * (h + 1)], NT_DIMS, preferred_element_type=F32)
        ql_sc[T * h:T * (h + 1), 0:C] = qlat.astype(ql_sc.dtype)
        ql_sc[T * h:T * (h + 1), C:C + LANE] = q_ref[:, 2 * LANE * h + LANE:2 * LANE * (h + 1)]
    ka_sc[0:P, 0:C] = cckv_ref[...].astype(ka_sc.dtype)
    ka_sc[0:P, C:C + LANE] = ckr_ref[...]
    ka_sc[P:P + T, 0:C] = nckv_ref[...]
    ka_sc[P:P + T, C:C + LANE] = nkr_ref[...]
    ka_sc[P + T:P + KPAD, :] = jnp.zeros((KPAD - T, C + LANE), ka_sc.dtype)
    ka = ka_sc[...]
    s = lax.dot_general(ql_sc[...], ka, NT_DIMS, preferred_element_type=F32)
    kpos = lax.broadcasted_iota(jnp.int32, s.shape, 1)
    qpos = P + lax.broadcasted_iota(jnp.int32, s.shape, 0) % T
    s = jnp.where((kpos < P + T) & (kpos // CHUNK <= qpos // CHUNK), s, NEG_BIG)
    m = jnp.max(s, axis=-1, keepdims=True)
    p = jnp.exp2(s - m)
    l = jnp.sum(p, axis=-1, keepdims=True)
    olat = (jnp.dot(p.astype(ka.dtype), ka[:, 0:C], preferred_element_type=F32) / l).astype(wuv_ref.dtype)
    for h in range(H):
        oh = jnp.dot(olat[T * h:T * (h + 1), :], wuv_ref[:, LANE * h:LANE * (h + 1)], preferred_element_type=F32)
        o_ref[:, LANE * h:LANE * (h + 1)] = oh.astype(o_ref.dtype)


def sample_attention(qfull, w_uk, w_uv, cache_ckv, cache_krd, layer, ckv_b, krc_b, ob, S, Bs, T, H):
    C = w_uk.shape[-2]
    P = cache_ckv.shape[2]
    KPAD = LANE
    assert T % 16 == 0 and S % T == 0 and T <= KPAD and w_uk.ndim == 3
    off = S // T
    return pl.pallas_call(
        functools.partial(_sattn_kernel, H=H, T=T, C=C, P=P, KPAD=KPAD),
        grid=(Bs,),
        in_specs=[pl.BlockSpec((T, 2 * LANE * H), lambda b: (b + off, 0)),
                  pl.BlockSpec((None, C, LANE * H), lambda b: (layer, 0, 0)),
                  pl.BlockSpec((None, C, LANE * H), lambda b: (layer, 0, 0)),
                  pl.BlockSpec((None, None, P, C), lambda b: (layer, b, 0, 0)),
                  pl.BlockSpec((None, None, P, LANE), lambda b: (layer, b, 0, 0)),
                  pl.BlockSpec((T, C), lambda b: (b + off, 0)),
                  pl.BlockSpec((T, LANE), lambda b: (b + off, 0)),
                  pl.BlockSpec(memory_space=pl.ANY)],
        out_specs=pl.BlockSpec((T, LANE * H), lambda b: (b + off, 0)),
        out_shape=jax.ShapeDtypeStruct(ob.shape, ob.dtype),
        scratch_shapes=[pltpu.VMEM((H * T, C + LANE), BF16), pltpu.VMEM((P + KPAD, C + LANE), BF16)],
        input_output_aliases={7: 0},
        compiler_params=_params("arbitrary"),
        name="sample_attention",
    )(qfull, w_uk, w_uv, cache_ckv, cache_krd, ckv_b, krc_b, ob)


def _xattn_kernel(*refs, heads, dh, scale, aliased):
    if aliased:
        q_ref, k_ref, v_ref, _, o_ref = refs
    else:
        q_ref, k_ref, v_ref, o_ref = refs
    for h in range(heads):
        sl = slice(dh * h, dh * (h + 1))
        kh = k_ref[:, sl].astype(BF16)
        vh = v_ref[:, sl].astype(BF16)
        s = lax.dot_general(q_ref[:, sl], kh, NT_DIMS, preferred_element_type=F32) * scale
        p = jnp.exp(s - jnp.max(s, axis=-1, keepdims=True))
        l = jnp.sum(p, axis=-1, keepdims=True)
        o_ref[:, sl] = (jnp.dot(p.astype(BF16), vh, preferred_element_type=F32) / l).astype(o_ref.dtype)


def cross_attention(qm, mem_k, mem_v, heads, S, Bs, T, layer=None, prev_out=None):
    R, W = qm.shape
    dh = W // heads
    prompt = prev_out is None
    bm = _pick(S, 512, 16) if prompt else T
    off = 0 if prompt else S // bm
    steps = S // bm if prompt else Bs
    if prompt:
        M = mem_k.shape[0]
        kv_spec = pl.BlockSpec((M, W), lambda i: (0, 0))
    else:
        M = mem_k.shape[2]
        kv_spec = pl.BlockSpec((None, None, M, W), lambda i: (layer, i, 0, 0))
    in_specs = [pl.BlockSpec((bm, W), lambda i: (i + off, 0)), kv_spec, kv_spec]
    args = [qm, mem_k, mem_v]
    aliases = {}
    if not prompt:
        in_specs.append(pl.BlockSpec(memory_space=pl.ANY))
        args.append(prev_out)
        aliases[3] = 0
    return pl.pallas_call(
        functools.partial(_xattn_kernel, heads=heads, dh=dh, scale=dh ** -0.5, aliased=not prompt),
        grid=(steps,),
        in_specs=in_specs,
        out_specs=pl.BlockSpec((bm, W), lambda i: (i + off, 0)),
        out_shape=jax.ShapeDtypeStruct((R, W), BF16),
        input_output_aliases=aliases,
        compiler_params=_params("arbitrary"),
        name="xattn_prompt" if prompt else "xattn_sample",
    )(*args)


def _rope_tables(S, Bs, T, P, half):
    pos = jnp.concatenate([jnp.arange(S, dtype=jnp.int32),
                           jnp.tile(P + jnp.arange(T, dtype=jnp.int32), Bs)])
    inv_freq = ROPE_BASE ** (-jnp.arange(half, dtype=F32) / half)
    ang = pos.astype(F32)[:, None] * inv_freq[None, :]
    cos, sin = jnp.cos(ang), jnp.sin(ang)
    return jnp.concatenate([cos, cos, -sin, sin], axis=-1)


def _step(x_prompt, x_sample, cache_mla_ckv, cache_mla_krope, cache_mem_k, cache_mem_v,
          state_conv_a, state_conv_ffn, mem_prompt,
          norm_mix, w_in, q_norm, kv_norm, w_uq, w_ukv, conv_a, w_out_a, w_o_mla, w_o,
          norm_xattn, mem_norm, w_mq, w_mk, w_mv, w_mo, norm_ffn, w_up, conv_ffn, w_down,
          norm_final):
    Bp, S, D = x_prompt.shape
    Bs, T, _ = x_sample.shape
    DEPTH = w_in.shape[0]
    P = cache_mla_ckv.shape[2]
    C = cache_mla_ckv.shape[3]
    ROPE = cache_mla_krope.shape[3]
    half = ROPE // 2
    A = conv_a.shape[2]
    Q = q_norm.shape[1]
    H = w_ukv.shape[2]
    NOPE = w_uq.shape[2] // H - ROPE
    V = w_ukv.shape[3] - NOPE
    MEM = mem_prompt.shape[1]
    MH, MD = cache_mem_k.shape[3], cache_mem_k.shape[4]
    F = w_down.shape[1]
    assert Bp == 1 and NOPE == LANE and V == LANE and 4 * half == LANE
    R = S + Bs * T
    scale = (NOPE + ROPE) ** -0.5

    lay = {"A": A, "Q": Q, "C": C, "b": 0, "c": A, "h": 2 * A, "q": 3 * A, "kv": 3 * A + Q,
           "kr": 3 * A + Q + C}
    used = 3 * A + Q + C + LANE
    g0 = -(-used // 1024) * 1024
    lay["ga"], lay["gb"] = g0, g0 + D
    kr0 = 3 * A + Q + C

    x = jnp.concatenate([x_prompt.reshape(S, D), x_sample.reshape(Bs * T, D)], axis=0)
    tk = _rope_tables(S, Bs, T, P, half)
    qscale = scale * LOG2E
    tq = tk * qscale
    zeros_a = jnp.zeros((1, 2, A), F32)
    zeros_f = jnp.zeros((1, 2, 2 * F), F32)
    cache_krd = jnp.concatenate([cache_mla_krope, cache_mla_krope], axis=-1).astype(BF16)
    cmk = cache_mem_k.reshape(DEPTH, Bs, MEM, MH * MD)
    cmv = cache_mem_v.reshape(DEPTH, Bs, MEM, MH * MD)
    mem = mem_prompt.reshape(MEM, D)

    bf = lambda w: w.astype(BF16)
    w_in_t = relayout_w_in(jnp.swapaxes(w_in, 1, 2), norm_mix, kr0, half, g0, 2 * D)
    wq = w_uq.reshape(DEPTH, Q, H, NOPE + ROPE)
    x1, x2 = bf(wq[..., NOPE:NOPE + half]), bf(wq[..., NOPE + half:])
    w_uq_b = jnp.concatenate([bf(wq[..., :NOPE]), x1, x2, x2, x1], axis=-1).reshape(DEPTH, Q, H * 2 * LANE)
    w_uk_b = bf(w_ukv[..., :NOPE]).reshape(DEPTH, C, H * NOPE)
    w_uv_b = bf(w_ukv[..., NOPE:]).reshape(DEPTH, C, H * V)
    w_out_a_b, w_o_mla_b, w_o_b = bf(w_out_a), bf(w_o_mla), bf(w_o)
    w_mq_b = bf(norm_xattn[:, :, None] * w_mq)
    w_mk_b, w_mv_b, w_mo_b = bf(w_mk), bf(w_mv), bf(w_mo)
    w_up_b, w_down_b = bf(norm_ffn[:, :, None] * w_up), bf(w_down)

    outs = {k: [] for k in ("ckv_p", "kr_p", "mk", "mv", "ca_p", "cf_p", "ckv_s", "kr_s", "ca_s", "cf_s")}
    xb, rinv = row_stats(x)
    for l in range(DEPTH):
        proj = matmul_nt(xb, w_in_t, rinv, F32, l, name="matmul_in")
        po = post_in(proj, lay, zeros_a, conv_a[l], q_norm[l], kv_norm[l], tk, S, Bs, T)
        ya_in, cq, ckv, ckv_b, krc, krc_b, utail = post_in(
            proj, lay, state_conv_a[l], conv_a[l], q_norm[l], kv_norm[l], tk, S, Bs, T, prev_outs=po)
        qfull = matmul_q(cq, w_uq_b, l, tq, qscale)
        k_pad, v_ext = matmul_kv(ckv_b, krc_b, w_uk_b, w_uv_b, l, S)
        ob = prompt_attention(qfull, k_pad, v_ext, S, R, H)
        ob = sample_attention(qfull, w_uk_b, w_uv_b, cache_mla_ckv, cache_krd, l, ckv_b, krc_b, ob, S, Bs, T, H)
        merged = matmul_merge(ya_in, w_out_a_b, ob, w_o_mla_b, l, proj, lay["ga"], lay["gb"])
        x, xb, rinv = matmul(merged, w_o_b, F32, res=x, stats=True, layer=l, bn_target=512, name="matmul_o")

        mn = rmsnorm_rows(mem, mem_norm[l], BF16)
        mk = matmul(mn, w_mk_b, F32, layer=l, name="matmul_mk")
        mv = matmul(mn, w_mv_b, F32, layer=l, name="matmul_mv")
        qm = matmul(xb, w_mq_b, BF16, rinv=rinv, layer=l, name="matmul_mq")
        om = cross_attention(qm, mk, mv, MH, S, Bs, T)
        om = cross_attention(qm, cmk, cmv, MH, S, Bs, T, layer=l, prev_out=om)
        x, xb, rinv = matmul(om, w_mo_b, F32, res=x, stats=True, layer=l, bn_target=512, name="matmul_mo")

        up = matmul(xb, w_up_b, F32, rinv=rinv, layer=l, bn_target=512, name="matmul_up")
        hmid = ffn_gate(up, zeros_f, conv_ffn[l], S, Bs, T)
        hmid = ffn_gate(up, state_conv_ffn[l], conv_ffn[l], S, Bs, T, prev_out=hmid)
        if l + 1 < DEPTH:
            x, xb, rinv = matmul(hmid, w_down_b, F32, res=x, stats=True, layer=l, bn_target=512, k_splits=2,
                                 name="matmul_down")
        else:
            x = matmul(hmid, w_down_b, F32, res=x, layer=l, bn_target=512, k_splits=2, name="matmul_down")

        n_pt = S // _pick(S, EW_ROW_TILE, 16)
        outs["ckv_p"].append(ckv[:S].reshape(Bp, S, C))
        outs["kr_p"].append(krc[:S, :ROPE].reshape(Bp, S, ROPE))
        outs["mk"].append(mk.reshape(Bp, MEM, MH, MD))
        outs["mv"].append(mv.reshape(Bp, MEM, MH, MD))
        outs["ca_p"].append(utail[n_pt * HALO_ROWS - 2:n_pt * HALO_ROWS].reshape(Bp, 2, A))
        outs["cf_p"].append(up[S - 2:S].reshape(Bp, 2, 2 * F))
        outs["ckv_s"].append(ckv[S:].reshape(Bs, T, C))
        outs["kr_s"].append(krc[S:, :ROPE].reshape(Bs, T, ROPE))
        outs["ca_s"].append(utail[n_pt * HALO_ROWS:].reshape(Bs, HALO_ROWS, A)[:, HALO_ROWS - 2:])
        outs["cf_s"].append(up[S:].reshape(Bs, T, 2 * F)[:, T - 2:])

    y_prompt = rmsnorm_rows(x, norm_final, F32, row0=0, nrows=S).reshape(Bp, S, D)
    y_sample = rmsnorm_rows(x, norm_final, F32, row0=S, nrows=Bs * T).reshape(Bs, T, D)
    st = {k: jnp.stack(v) for k, v in outs.items()}
    return (y_prompt, y_sample, st["ckv_p"], st["kr_p"], st["mk"], st["mv"], st["ca_p"], st["cf_p"],
            st["ckv_s"], st["kr_s"], st["ca_s"], st["cf_s"])


def kernel(x_prompt, x_sample, cache_mla_ckv, cache_mla_krope, cache_mem_k, cache_mem_v, state_conv_a, state_conv_ffn, mem_prompt, norm_mix, w_in, q_norm, kv_norm, w_uq, w_ukv, conv_a, w_out_a, w_o_mla, w_o, norm_xattn, mem_norm, w_mq, w_mk, w_mv, w_mo, norm_ffn, w_up, conv_ffn, w_down, norm_final):
    return _step(x_prompt, x_sample, cache_mla_ckv, cache_mla_krope, cache_mem_k, cache_mem_v,
                 state_conv_a, state_conv_ffn, mem_prompt,
                 norm_mix, w_in, q_norm, kv_norm, w_uq, w_ukv, conv_a, w_out_a, w_o_mla, w_o,
                 norm_xattn, mem_norm, w_mq, w_mk, w_mv, w_mo, norm_ffn, w_up, conv_ffn, w_down,
                 norm_final)
```

```python
import functools

import jax
import jax.numpy as jnp
from jax import lax
from jax.experimental import pallas as pl
from jax.experimental.pallas import tpu as pltpu

F32 = jnp.float32
BF16 = jnp.bfloat16

EPS = 1e-6
CHUNK = 64
ROPE_BASE = 10000.0
NEG_BIG = -1e30

VMEM_LIMIT_BYTES = 60 * 1024 * 1024
LANE = 128
SUBLANE = 8
HALO_ROWS = 8
MM_ROW_TILE = 1040
EW_ROW_TILE = 256
FA_BLOCK_Q = 4096
FA_SUB_ROWS = 1024
FA_BLOCK_K = 512
FA_BLOCK_K_MAIN = 2048
LOG2E = 1.4426950408889634
NT_DIMS = (((1,), (1,)), ((), ()))


def _pick(n, target, mult):
    best = None
    for d in range(mult, min(n, target) + 1, mult):
        if n % d == 0:
            best = d
    return best or n


def _params(*sem):
    return pltpu.CompilerParams(dimension_semantics=sem, vmem_limit_bytes=VMEM_LIMIT_BYTES)


def _rms(x, g):
    return x * lax.rsqrt(jnp.mean(x * x, axis=-1, keepdims=True) + EPS) * g


def _rms_kernel(x_ref, g_ref, o_ref):
    o_ref[...] = _rms(x_ref[...].astype(F32), g_ref[...]).astype(o_ref.dtype)


def rmsnorm_rows(x, g, out_dtype, row0=0, nrows=None, bm_target=512):
    R, C = x.shape
    nrows = R - row0 if nrows is None else nrows
    bm = _pick(nrows, bm_target, 16)
    assert row0 % bm == 0
    off = row0 // bm
    return pl.pallas_call(
        _rms_kernel,
        grid=(nrows // bm,),
        in_specs=[pl.BlockSpec((bm, C), lambda i: (i + off, 0)),
                  pl.BlockSpec((1, C), lambda i: (0, 0))],
        out_specs=pl.BlockSpec((bm, C), lambda i: (i, 0)),
        out_shape=jax.ShapeDtypeStruct((nrows, C), out_dtype),
        compiler_params=_params("parallel"),
        name="rmsnorm",
    )(x, g.reshape(1, C).astype(F32))


def _scale_rows(acc, rinv):
    return jnp.concatenate([acc[:, LANE * c:LANE * (c + 1)] * rinv for c in range(acc.shape[1] // LANE)], axis=1)


def _mm_kernel(a_ref, w_ref, o_ref):
    o_ref[...] = jnp.dot(a_ref[...], w_ref[...], preferred_element_type=F32).astype(o_ref.dtype)


def _mm_rinv_kernel(a_ref, w_ref, rinv_ref, o_ref):
    acc = jnp.dot(a_ref[...], w_ref[...], preferred_element_type=F32)
    o_ref[...] = _scale_rows(acc, rinv_ref[...]).astype(o_ref.dtype)


def _mm_res_kernel(a_ref, w_ref, r_ref, o_ref):
    acc = jnp.dot(a_ref[...], w_ref[...], preferred_element_type=F32)
    o_ref[...] = (r_ref[...] + acc).astype(o_ref.dtype)


def _emit_row_stats(y, xb_ref, rinv_ref, j, nj):
    xb_ref[...] = y.astype(xb_ref.dtype)
    sq = y * y
    part = sq[:, 0:LANE]
    for c in range(1, y.shape[1] // LANE):
        part = part + sq[:, LANE * c:LANE * (c + 1)]

    @pl.when(j == 0)
    def _():
        rinv_ref[...] = part

    @pl.when(j > 0)
    def _():
        rinv_ref[...] += part

    @pl.when(j == nj - 1)
    def _():
        tot = jnp.sum(rinv_ref[...], axis=-1, keepdims=True)
        rinv_ref[...] = jnp.broadcast_to(lax.rsqrt(tot * (1.0 / (nj * y.shape[1])) + EPS), rinv_ref.shape)


def _mm_res_stats_kernel(a_ref, w_ref, r_ref, o_ref, xb_ref, rinv_ref, *, nj):
    y = r_ref[...] + jnp.dot(a_ref[...], w_ref[...], preferred_element_type=F32)
    o_ref[...] = y
    _emit_row_stats(y, xb_ref, rinv_ref, pl.program_id(1), nj)


def _row_stats_kernel(x_ref, xb_ref, rinv_ref):
    x = x_ref[...]
    xb_ref[...] = x.astype(xb_ref.dtype)
    rinv = lax.rsqrt(jnp.mean(x * x, axis=-1, keepdims=True) + EPS)
    rinv_ref[...] = jnp.broadcast_to(rinv, rinv_ref.shape)


def row_stats(x):
    R, C = x.shape
    bm = _pick(R, 512, 16)
    return pl.pallas_call(
        _row_stats_kernel,
        grid=(R // bm,),
        in_specs=[pl.BlockSpec((bm, C), lambda i: (i, 0))],
        out_specs=[pl.BlockSpec((bm, C), lambda i: (i, 0)), pl.BlockSpec((bm, LANE), lambda i: (i, 0))],
        out_shape=[jax.ShapeDtypeStruct((R, C), BF16), jax.ShapeDtypeStruct((R, LANE), F32)],
        compiler_params=_params("parallel"),
        name="row_stats",
    )(x)


def _wspec(w, layer, bn, kb=None, ks=0):
    kb = w.shape[-2] if kb is None else kb
    if w.ndim == 3:
        return pl.BlockSpec((None, kb, bn), lambda i, j: (layer, ks, j))
    return pl.BlockSpec((kb, bn), lambda i, j: (ks, j))


def matmul(a, w, out_dtype, res=None, rinv=None, stats=False, layer=None, bm_target=MM_ROW_TILE,
           bn_target=1024, k_splits=1, name="matmul"):
    if k_splits > 1:
        assert res is not None and rinv is None and a.shape[1] % (k_splits * LANE) == 0
        for ks in range(k_splits):
            last = ks == k_splits - 1
            res = _matmul_call(a, w, out_dtype, res, None, stats and last, layer, bm_target, bn_target,
                               a.shape[1] // k_splits, ks, name)
        return res
    return _matmul_call(a, w, out_dtype, res, rinv, stats, layer, bm_target, bn_target, a.shape[1], 0, name)


def _matmul_call(a, w, out_dtype, res, rinv, stats, layer, bm_target, bn_target, K, ks, name):
    M = a.shape[0]
    N = w.shape[-1]
    bm = _pick(M, bm_target, 16)
    bn = _pick(N, bn_target, LANE)
    in_specs = [pl.BlockSpec((bm, K), lambda i, j: (i, ks)), _wspec(w, layer, bn, K, ks)]
    args = [a, w]
    body = _mm_kernel
    out_specs = pl.BlockSpec((bm, bn), lambda i, j: (i, j))
    out_shape = jax.ShapeDtypeStruct((M, N), out_dtype)
    sem = ("parallel", "parallel")
    if res is not None:
        assert rinv is None
        in_specs.append(pl.BlockSpec((bm, bn), lambda i, j: (i, j)))
        args.append(res)
        body = _mm_res_kernel
        if stats:
            assert out_dtype == F32
            body = functools.partial(_mm_res_stats_kernel, nj=N // bn)
            out_specs = [out_specs, pl.BlockSpec((bm, bn), lambda i, j: (i, j)),
                         pl.BlockSpec((bm, LANE), lambda i, j: (i, 0))]
            out_shape = [out_shape, jax.ShapeDtypeStruct((M, N), BF16), jax.ShapeDtypeStruct((M, LANE), F32)]
            sem = ("parallel", "arbitrary")
    elif rinv is not None:
        in_specs.append(pl.BlockSpec((bm, LANE), lambda i, j: (i, 0)))
        args.append(rinv)
        body = _mm_rinv_kernel
    return pl.pallas_call(
        body,
        grid=(M // bm, N // bn),
        in_specs=in_specs,
        out_specs=out_specs,
        out_shape=out_shape,
        compiler_params=_params(*sem),
        name=name,
    )(*args)


def _mm_nt_rinv_kernel(a_ref, wt_ref, rinv_ref, o_ref):
    acc = lax.dot_general(a_ref[...], wt_ref[...], NT_DIMS, preferred_element_type=F32)
    o_ref[...] = _scale_rows(acc, rinv_ref[...]).astype(o_ref.dtype)


def matmul_nt(a, wt, rinv, out_dtype, layer, bn_target=1024, name="matmul_nt"):
    M, K = a.shape
    N = wt.shape[1]
    bm = _pick(M, MM_ROW_TILE, 16)
    bn = _pick(N, bn_target, LANE)
    return pl.pallas_call(
        _mm_nt_rinv_kernel,
        grid=(M // bm, N // bn),
        in_specs=[pl.BlockSpec((bm, K), lambda i, j: (i, 0)),
                  pl.BlockSpec((None, bn, K), lambda i, j: (layer, j, 0)),
                  pl.BlockSpec((bm, LANE), lambda i, j: (i, 0))],
        out_specs=pl.BlockSpec((bm, bn), lambda i, j: (i, j)),
        out_shape=jax.ShapeDtypeStruct((M, N), out_dtype),
        compiler_params=_params("parallel", "parallel"),
        name=name,
    )(a, wt, rinv)


def _win_kernel(w_ref, g_ref, o_ref, *, kr0, half, g0):
    n_gate = o_ref.shape[0] - g0
    dt = o_ref.dtype
    g = g_ref[...]
    o_ref[0:kr0, :] = (w_ref[0:kr0, :] * g).astype(dt)
    x1 = (w_ref[kr0:kr0 + half, :] * g).astype(dt)
    x2 = (w_ref[kr0 + half:kr0 + 2 * half, :] * g).astype(dt)
    o_ref[kr0:kr0 + half, :] = x1
    o_ref[kr0 + half:kr0 + 2 * half, :] = x2
    o_ref[kr0 + 2 * half:kr0 + 3 * half, :] = x2
    o_ref[kr0 + 3 * half:kr0 + 4 * half, :] = x1
    o_ref[kr0 + 4 * half:g0, :] = jnp.zeros((g0 - kr0 - 4 * half, o_ref.shape[1]), dt)
    o_ref[g0:, :] = (w_ref[kr0 + 2 * half:kr0 + 2 * half + n_gate, :] * g).astype(dt)


def relayout_w_in(w_in_t, gain, kr0, half, g0, n_gate):
    DEPTH, NC, K = w_in_t.shape
    assert kr0 % 16 == 0 and half % 16 == 0 and g0 % 16 == 0 and kr0 + 2 * half + n_gate == NC
    return pl.pallas_call(
        functools.partial(_win_kernel, kr0=kr0, half=half, g0=g0),
        grid=(DEPTH, K // LANE),
        in_specs=[pl.BlockSpec((None, NC, LANE), lambda l, i: (l, 0, i)),
                  pl.BlockSpec((None, 1, LANE), lambda l, i: (l, 0, i))],
        out_specs=pl.BlockSpec((None, g0 + n_gate, LANE), lambda l, i: (l, 0, i)),
        out_shape=jax.ShapeDtypeStruct((DEPTH, g0 + n_gate, K), BF16),
        compiler_params=_params("parallel", "parallel"),
        name="relayout_w_in",
    )(w_in_t, gain.reshape(DEPTH, 1, K))


def _mm_q_kernel(a_ref, w_ref, t_ref, o_ref, *, heads, scale):
    acc = jnp.dot(a_ref[...], w_ref[...], preferred_element_type=F32)
    t = t_ref[...]
    for h in range(heads):
        c0 = 2 * LANE * h
        o_ref[:, c0:c0 + LANE] = (acc[:, c0:c0 + LANE] * scale).astype(o_ref.dtype)
        o_ref[:, c0 + LANE:c0 + 2 * LANE] = (acc[:, c0 + LANE:c0 + 2 * LANE] * t).astype(o_ref.dtype)


def matmul_q(a, w, layer, tq, scale, heads_per_tile=4):
    M, K = a.shape
    N = w.shape[-1]
    bm = _pick(M, MM_ROW_TILE, 16)
    bn = 2 * LANE * heads_per_tile
    assert N % bn == 0
    return pl.pallas_call(
        functools.partial(_mm_q_kernel, heads=heads_per_tile, scale=scale),
        grid=(M // bm, N // bn),
        in_specs=[pl.BlockSpec((bm, K), lambda i, j: (i, 0)),
                  _wspec(w, layer, bn),
                  pl.BlockSpec((bm, LANE), lambda i, j: (i, 0))],
        out_specs=pl.BlockSpec((bm, bn), lambda i, j: (i, j)),
        out_shape=jax.ShapeDtypeStruct((M, N), BF16),
        compiler_params=_params("parallel", "parallel"),
        name="matmul_q",
    )(a, w, tq)


def _mm_kv_kernel(a_ref, wk_ref, wv_ref, k_ref, v_ref):
    a = a_ref[...]
    k_ref[...] = jnp.dot(a, wk_ref[...], preferred_element_type=F32).astype(k_ref.dtype)
    v_ref[...] = jnp.dot(a, wv_ref[...], preferred_element_type=F32).astype(v_ref.dtype)


def matmul_kv(ckv_b, w_uk, w_uv, layer, S, heads_per_tile=4):
    C = ckv_b.shape[1]
    N = w_uk.shape[-1]
    bm = _pick(S, 1024, 16)
    bn = LANE * heads_per_tile
    assert N % bn == 0
    return pl.pallas_call(
        _mm_kv_kernel,
        grid=(S // bm, N // bn),
        in_specs=[pl.BlockSpec((bm, C), lambda i, j: (i, 0)),
                  _wspec(w_uk, layer, bn),
                  _wspec(w_uv, layer, bn)],
        out_specs=[pl.BlockSpec((bm, bn), lambda i, j: (i, j)),
                   pl.BlockSpec((bm, bn), lambda i, j: (i, j))],
        out_shape=[jax.ShapeDtypeStruct((S, N), BF16),
                   jax.ShapeDtypeStruct((S, N), BF16)],
        compiler_params=_params("parallel", "parallel"),
        name="matmul_kv",
    )(ckv_b, w_uk, w_uv)


def _sigmoid(x):
    return 1.0 / (1.0 + jnp.exp(-x))


def _mm_merge_kernel(a1_ref, w1_ref, a2_ref, w2_ref, ga_ref, gb_ref, o_ref):
    ya = jnp.dot(a1_ref[...], w1_ref[...], preferred_element_type=F32)
    yb = jnp.dot(a2_ref[...], w2_ref[...], preferred_element_type=F32)
    o_ref[...] = (_sigmoid(ga_ref[...]) * ya + _sigmoid(gb_ref[...]) * yb).astype(o_ref.dtype)


def matmul_merge(ya_in, w_out_a, ob, w_o_mla, layer, proj, ga_col0, gb_col0):
    M, K1 = ya_in.shape
    K2 = ob.shape[1]
    N = w_out_a.shape[-1]
    bm = _pick(M, MM_ROW_TILE, 16)
    bn = _pick(N, 512, LANE)
    assert ga_col0 % bn == 0 and gb_col0 % bn == 0
    ja, jb = ga_col0 // bn, gb_col0 // bn
    return pl.pallas_call(
        _mm_merge_kernel,
        grid=(M // bm, N // bn),
        in_specs=[pl.BlockSpec((bm, K1), lambda i, j: (i, 0)),
                  _wspec(w_out_a, layer, bn),
                  pl.BlockSpec((bm, K2), lambda i, j: (i, 0)),
                  _wspec(w_o_mla, layer, bn),
                  pl.BlockSpec((bm, bn), lambda i, j: (i, j + ja)),
                  pl.BlockSpec((bm, bn), lambda i, j: (i, j + jb))],
        out_specs=pl.BlockSpec((bm, bn), lambda i, j: (i, j)),
        out_shape=jax.ShapeDtypeStruct((M, N), BF16),
        compiler_params=_params("parallel", "parallel"),
        name="matmul_merge",
    )(ya_in, w_out_a, ob, w_o_mla, proj, proj)


def _conv3(u, prev8, w, x_sc):
    rows = u.shape[0]
    x_sc[0:HALO_ROWS, :] = prev8
    x_sc[HALO_ROWS:, :] = u
    return (w[0:1] * x_sc[HALO_ROWS - 2:HALO_ROWS - 2 + rows, :]
            + w[1:2] * x_sc[HALO_ROWS - 1:HALO_ROWS - 1 + rows, :] + w[2:3] * u)


def _halo(prev_u, st, first):
    sub = lax.broadcasted_iota(jnp.int32, (HALO_ROWS, 1), 0)
    st8 = jnp.where(sub == HALO_ROWS - 2, st[0:1], jnp.where(sub == HALO_ROWS - 1, st[1:2], 0.0))
    if prev_u is None:
        return st8
    return jnp.where(first, st8, prev_u)


def _postin_kernel(*refs, has_prev):
    x_sc = refs[-1]
    refs = refs[:-1]
    if has_prev:
        (b_ref, c_ref, h_ref, cp_ref, hp_ref, ql_ref, kv_ref, kr_ref, st_ref, taps_ref, qn_ref, kvn_ref,
         t_ref, ya_ref, cq_ref, ckv_ref, ckvb_ref, krc_ref, krcb_ref, ut_ref) = refs
    else:
        (b_ref, c_ref, h_ref, ql_ref, kv_ref, kr_ref, st_ref, taps_ref, qn_ref, kvn_ref,
         t_ref, ya_ref, cq_ref, ckv_ref, ckvb_ref, krc_ref, krcb_ref, ut_ref) = refs
    first = pl.program_id(0) == 0
    rows = ya_ref.shape[0]

    def column(c, carry):
        cs = pl.ds(pl.multiple_of(c * LANE, LANE), LANE)
        u = c_ref[:, cs] * h_ref[:, cs]
        prev8 = _halo(cp_ref[:, cs] * hp_ref[:, cs] if has_prev else None, st_ref[:, cs], first)
        ya_ref[:, cs] = (b_ref[:, cs] * _conv3(u, prev8, taps_ref[:, cs], x_sc)).astype(ya_ref.dtype)
        ut_ref[:, cs] = u[rows - HALO_ROWS:, :]
        return carry

    lax.fori_loop(0, ya_ref.shape[1] // LANE, column, 0, unroll=2)
    cq_ref[...] = _rms(ql_ref[...], qn_ref[...]).astype(cq_ref.dtype)
    ckv = _rms(kv_ref[...], kvn_ref[...])
    ckv_ref[...] = ckv
    ckvb_ref[...] = ckv.astype(ckvb_ref.dtype)
    t = kr_ref[...] * t_ref[...]
    krc = t + pltpu.roll(t, LANE // 2, 1)
    krc_ref[...] = krc
    krcb_ref[...] = krc.astype(krcb_ref.dtype)


def post_in(proj, lay, state, taps, q_norm, kv_norm, tk, S, Bs, T, prev_outs=None):
    R = proj.shape[0]
    A, Q, C = lay["A"], lay["Q"], lay["C"]
    prompt = prev_outs is None
    bm = _pick(S, EW_ROW_TILE, 16) if prompt else T
    assert bm % 16 == 0 and S % bm == 0
    off = 0 if prompt else S // bm
    steps = S // bm if prompt else Bs
    rb = bm // HALO_ROWS

    def col(width, col0):
        assert col0 % width == 0
        return pl.BlockSpec((bm, width), lambda i, c=col0 // width: (i + off, c))

    in_specs = [col(A, lay["b"]), col(A, lay["c"]), col(A, lay["h"])]
    args = [proj, proj, proj]
    if prompt:
        for name in ("c", "h"):
            in_specs.append(pl.BlockSpec((HALO_ROWS, A),
                                         lambda i, c=lay[name] // A: (jnp.maximum(i * rb - 1, 0), c)))
            args.append(proj)
    in_specs += [col(Q, lay["q"]), col(C, lay["kv"]), col(LANE, lay["kr"]),
                 pl.BlockSpec((None, 2, A), (lambda i: (0, 0, 0)) if prompt else (lambda i: (i, 0, 0))),
                 pl.BlockSpec((3, A), lambda i: (0, 0)),
                 pl.BlockSpec((1, Q), lambda i: (0, 0)),
                 pl.BlockSpec((1, C), lambda i: (0, 0)),
                 pl.BlockSpec((bm, LANE), lambda i: (i + off, 0))]
    args += [proj, proj, proj, state, taps, q_norm.reshape(1, Q), kv_norm.reshape(1, C), tk]
    out_widths = [(A, BF16), (Q, BF16), (C, F32), (C, BF16), (LANE, F32), (LANE, BF16)]
    out_specs = [pl.BlockSpec((bm, w), lambda i: (i + off, 0)) for w, _ in out_widths]
    out_shape = [jax.ShapeDtypeStruct((R, w), dt) for w, dt in out_widths]
    n_pt = S // _pick(S, EW_ROW_TILE, 16)
    toff = 0 if prompt else n_pt
    out_specs.append(pl.BlockSpec((HALO_ROWS, A), lambda i: (i + toff, 0)))
    out_shape.append(jax.ShapeDtypeStruct(((n_pt + Bs) * HALO_ROWS, A), F32))
    aliases = {}
    if not prompt:
        n_in = len(args)
        for k, arr in enumerate(prev_outs):
            in_specs.append(pl.BlockSpec(memory_space=pl.ANY))
            args.append(arr)
            aliases[n_in + k] = k
    return pl.pallas_call(
        functools.partial(_postin_kernel_aliased if not prompt else _postin_kernel, has_prev=prompt),
        grid=(steps,),
        in_specs=in_specs,
        out_specs=out_specs,
        out_shape=out_shape,
        scratch_shapes=[pltpu.VMEM((bm + HALO_ROWS, LANE), F32)],
        input_output_aliases=aliases,
        compiler_params=_params("arbitrary"),
        name="post_in_prompt" if prompt else "post_in_sample",
    )(*args)


def _postin_kernel_aliased(*refs, has_prev):
    n_out = 7
    n_in = len(refs) - 2 * n_out - 1
    _postin_kernel(*refs[:n_in], *refs[n_in + n_out:], has_prev=has_prev)


def _ffn_kernel(*refs, has_prev, aliased):
    xa_sc, xg_sc = refs[-2:]
    refs = refs[:-2]
    if aliased:
        refs = refs[:-2] + refs[-1:]
    if has_prev:
        a_ref, g_ref, ap_ref, gp_ref, sa_ref, sg_ref, wa_ref, wg_ref, o_ref = refs
    else:
        a_ref, g_ref, sa_ref, sg_ref, wa_ref, wg_ref, o_ref = refs
    first = pl.program_id(0) == 0

    def column(c, carry):
        cs = pl.ds(pl.multiple_of(c * LANE, LANE), LANE)

        def conv(x_ref, p_ref, s_ref, w_ref, x_sc):
            prev8 = _halo(None if p_ref is None else p_ref[:, cs], s_ref[:, cs], first)
            return _conv3(x_ref[:, cs], prev8, w_ref[:, cs], x_sc)

        a = conv(a_ref, ap_ref if has_prev else None, sa_ref, wa_ref, xa_sc)
        g = conv(g_ref, gp_ref if has_prev else None, sg_ref, wg_ref, xg_sc)
        o_ref[:, cs] = (g * a / (1.0 + jnp.exp2(g * (-LOG2E)))).astype(o_ref.dtype)
        return carry

    lax.fori_loop(0, o_ref.shape[1] // LANE, column, 0, unroll=2)


def ffn_gate(up, state, taps, S, Bs, T, prev_out=None):
    R, F2 = up.shape
    F = F2 // 2
    prompt = prev_out is None
    bm = _pick(S, EW_ROW_TILE, 16) if prompt else T
    bc = _pick(F, 5504, LANE)
    nc = F // bc
    off = 0 if prompt else S // bm
    steps = S // bm if prompt else Bs
    rb = bm // HALO_ROWS
    in_specs = [pl.BlockSpec((bm, bc), lambda i, j: (i + off, j)),
                pl.BlockSpec((bm, bc), lambda i, j: (i + off, j + nc))]
    args = [up, up]
    if prompt:
        in_specs += [pl.BlockSpec((HALO_ROWS, bc), lambda i, j: (jnp.maximum(i * rb - 1, 0), j)),
                     pl.BlockSpec((HALO_ROWS, bc), lambda i, j: (jnp.maximum(i * rb - 1, 0), j + nc))]
        args += [up, up]
    bsel = (lambda i: 0) if prompt else (lambda i: i)
    in_specs += [pl.BlockSpec((None, 2, bc), lambda i, j: (bsel(i), 0, j)),
                 pl.BlockSpec((None, 2, bc), lambda i, j: (bsel(i), 0, j + nc)),
                 pl.BlockSpec((3, bc), lambda i, j: (0, j)),
                 pl.BlockSpec((3, bc), lambda i, j: (0, j + nc))]
    args += [state, state, taps, taps]
    aliases = {}
    if not prompt:
        in_specs.append(pl.BlockSpec(memory_space=pl.ANY))
        args.append(prev_out)
        aliases[len(args) - 1] = 0
    return pl.pallas_call(
        functools.partial(_ffn_kernel, has_prev=prompt, aliased=not prompt),
        grid=(steps, nc),
        in_specs=in_specs,
        out_specs=pl.BlockSpec((bm, bc), lambda i, j: (i + off, j)),
        out_shape=jax.ShapeDtypeStruct((R, F), BF16),
        scratch_shapes=[pltpu.VMEM((bm + HALO_ROWS, LANE), F32), pltpu.VMEM((bm + HALO_ROWS, LANE), F32)],
        input_output_aliases=aliases,
        compiler_params=_params("arbitrary", "arbitrary"),
        name="ffn_gate_prompt" if prompt else "ffn_gate_sample",
    )(*args)


def _fa_kernel(q_ref, k_ref, kr_ref, v_ref, o_ref, m_sc, acc_sc, *, bq, bk, bk_main, sub):
    i = pl.program_id(1)
    m_sc[...] = jnp.full(m_sc.shape, NEG_BIG, F32)
    acc_sc[...] = jnp.zeros(acc_sc.shape, F32)
    chains = bq // sub

    def chain_block(row0, nrows, k0, bk, local_k0):
        lane_tiles = bk // LANE
        rows = slice(row0, row0 + nrows)
        ks = pl.ds(k0, bk)
        kj = jnp.concatenate([k_ref[ks, :], kr_ref[ks, :]], axis=1)
        vj = jnp.concatenate([v_ref[ks, :], jnp.ones((bk, LANE), v_ref.dtype)], axis=1)
        s = lax.dot_general(q_ref[rows, :], kj, NT_DIMS, preferred_element_type=F32)
        if local_k0 is not None:
            kchunk = (lax.broadcasted_iota(jnp.int32, (nrows, bk), 1) + local_k0) // CHUNK
            qchunk = (lax.broadcasted_iota(jnp.int32, (nrows, bk), 0) + row0) // CHUNK
            s = jnp.where(kchunk <= qchunk, s, NEG_BIG)
        cols = [s[:, LANE * c:LANE * (c + 1)] for c in range(lane_tiles)]
        m_cur = cols[0]
        for c in cols[1:]:
            m_cur = jnp.maximum(m_cur, c)
        m_prev = m_sc[rows, :]
        m_new = jnp.maximum(m_prev, jnp.max(m_cur, axis=-1, keepdims=True))
        alpha = jnp.exp2(m_prev - m_new)
        p = jnp.concatenate([jnp.exp2(c - m_new) for c in cols], axis=1).astype(vj.dtype)
        pv = jnp.dot(p, vj, preferred_element_type=F32)
        acc_sc[rows, :] = jnp.concatenate([alpha, alpha], axis=1) * acc_sc[rows, :] + pv
        m_sc[rows, :] = m_new

    def body(j, carry):
        k0 = pl.multiple_of(j * bk_main, bk_main)
        for b in range(bk_main // bk):
            for r in range(chains):
                chain_block(sub * r, sub, k0 + b * bk, bk, None)
        return carry

    lax.fori_loop(0, i * (bq // bk_main), body, 0)
    t0 = pl.multiple_of(i * bq, bq)
    for b in range(bq // bk):
        for r in range(chains):
            lo = max(sub * r, b * bk)
            if lo < sub * (r + 1):
                chain_block(lo, sub * (r + 1) - lo, t0 + b * bk, bk,
                            None if (b + 1) * bk <= sub * r else b * bk)
    acc = acc_sc[...]
    o_ref[...] = (acc[:, :LANE] / acc[:, LANE:]).astype(o_ref.dtype)


def prompt_attention(q_pad, k_nope, krc_b, v, S, R, H):
    bq = _pick(S, FA_BLOCK_Q, CHUNK)
    sub = _pick(bq, FA_SUB_ROWS, CHUNK)
    bk = _pick(sub, FA_BLOCK_K, LANE)
    bk_main = _pick(bq, FA_BLOCK_K_MAIN, bk)
    assert bq % bk_main == 0 and sub % bk == 0 and S % bq == 0
    return pl.pallas_call(
        functools.partial(_fa_kernel, bq=bq, bk=bk, bk_main=bk_main, sub=sub),
        grid=(H, S // bq),
        in_specs=[pl.BlockSpec((bq, 2 * LANE), lambda h, i: (i, h)),
                  pl.BlockSpec((S, LANE), lambda h, i: (0, h)),
                  pl.BlockSpec((S, LANE), lambda h, i: (0, 0)),
                  pl.BlockSpec((S, LANE), lambda h, i: (0, h))],
        out_specs=pl.BlockSpec((bq, LANE), lambda h, i: (i, h)),
        out_shape=jax.ShapeDtypeStruct((R, H * LANE), BF16),
        scratch_shapes=[pltpu.VMEM((bq, LANE), F32), pltpu.VMEM((bq, 2 * LANE), F32)],
        compiler_params=_params("parallel", "arbitrary"),
        name="prompt_attention",
    )(q_pad, k_nope, krc_b, v)


def _sattn_kernel(q_ref, wuk_ref, wuv_ref, cckv_ref, ckr_ref, nckv_ref, nkr_ref, ob_ref, o_ref,
                  ql_sc, ka_sc, *, H, T, C, P, KPAD):
    del ob_ref
    for h in range(H):
        qn = q_ref[:, 2 * LANE * h:2 * LANE * h + LANE]
        qlat = lax.dot_general(qn, wuk_ref[:, LANE * h:LANE * (h + 1)], NT_DIMS, preferred_element_type=F32)
        ql_sc[T * h:T * (h + 1), 0:C] = qlat.astype(ql_sc.dtype)
        ql_sc[T * h:T * (h + 1), C:C + LANE] = q_ref[:, 2 * LANE * h + LANE:2 * LANE * (h + 1)]
    ka_sc[0:P, 0:C] = cckv_ref[...].astype(ka_sc.dtype)
    ka_sc[0:P, C:C + LANE] = ckr_ref[...]
    ka_sc[P:P + T, 0:C] = nckv_ref[...]
    ka_sc[P:P + T, C:C + LANE] = nkr_ref[...]
    ka_sc[P + T:P + KPAD, :] = jnp.zeros((KPAD - T, C + LANE), ka_sc.dtype)
    ka = ka_sc[...]
    s = lax.dot_general(ql_sc[...], ka, NT_DIMS, preferred_element_type=F32)
    kpos = lax.broadcasted_iota(jnp.int32, s.shape, 1)
    qpos = P + lax.broadcasted_iota(jnp.int32, s.shape, 0) % T
    s = jnp.where((kpos < P + T) & (kpos // CHUNK <= qpos // CHUNK), s, NEG_BIG)
    m = jnp.max(s, axis=-1, keepdims=True)
    p = jnp.exp2(s - m)
    l = jnp.sum(p, axis=-1, keepdims=True)
    olat = (jnp.dot(p.astype(ka.dtype), ka[:, 0:C], preferred_element_type=F32) / l).astype(wuv_ref.dtype)
    for h in range(H):
        oh = jnp.dot(olat[T * h:T * (h + 1), :], wuv_ref[:, LANE * h:LANE * (h + 1)], preferred_element_type=F32)
        o_ref[:, LANE * h:LANE * (h + 1)] = oh.astype(o_ref.dtype)


def sample_attention(qfull, w_uk, w_uv, cache_ckv, cache_krd, layer, ckv_b, krc_b, ob, S, Bs, T, H):
    C = w_uk.shape[-2]
    P = cache_ckv.shape[2]
    KPAD = LANE
    assert T % 16 == 0 and S % T == 0 and T <= KPAD and w_uk.ndim == 3
    off = S // T
    return pl.pallas_call(
        functools.partial(_sattn_kernel, H=H, T=T, C=C, P=P, KPAD=KPAD),
        grid=(Bs,),
        in_specs=[pl.BlockSpec((T, 2 * LANE * H), lambda b: (b + off, 0)),
                  pl.BlockSpec((None, C, LANE * H), lambda b: (layer, 0, 0)),
                  pl.BlockSpec((None, C, LANE * H), lambda b: (layer, 0, 0)),
                  pl.BlockSpec((None, None, P, C), lambda b: (layer, b, 0, 0)),
                  pl.BlockSpec((None, None, P, LANE), lambda b: (layer, b, 0, 0)),
                  pl.BlockSpec((T, C), lambda b: (b + off, 0)),
                  pl.BlockSpec((T, LANE), lambda b: (b + off, 0)),
                  pl.BlockSpec(memory_space=pl.ANY)],
        out_specs=pl.BlockSpec((T, LANE * H), lambda b: (b + off, 0)),
        out_shape=jax.ShapeDtypeStruct(ob.shape, ob.dtype),
        scratch_shapes=[pltpu.VMEM((H * T, C + LANE), BF16), pltpu.VMEM((P + KPAD, C + LANE), BF16)],
        input_output_aliases={7: 0},
        compiler_params=_params("arbitrary"),
        name="sample_attention",
    )(qfull, w_uk, w_uv, cache_ckv, cache_krd, ckv_b, krc_b, ob)


def _xattn_kernel(*refs, heads, dh, scale, aliased):
    if aliased:
        q_ref, k_ref, v_ref, _, o_ref = refs
    else:
        q_ref, k_ref, v_ref, o_ref = refs
    for h in range(heads):
        sl = slice(dh * h, dh * (h + 1))
        kh = k_ref[:, sl].astype(BF16)
        vh = v_ref[:, sl].astype(BF16)
        s = lax.dot_general(q_ref[:, sl], kh, NT_DIMS, preferred_element_type=F32) * scale
        p = jnp.exp(s - jnp.max(s, axis=-1, keepdims=True))
        l = jnp.sum(p, axis=-1, keepdims=True)
        o_ref[:, sl] = (jnp.dot(p.astype(BF16), vh, preferred_element_type=F32) / l).astype(o_ref.dtype)


def cross_attention(qm, mem_k, mem_v, heads, S, Bs, T, layer=None, prev_out=None):
    R, W = qm.shape
    dh = W // heads
    prompt = prev_out is None
    bm = _pick(S, 512, 16) if prompt else T
    off = 0 if prompt else S // bm
    steps = S // bm if prompt else Bs
    if prompt:
        M = mem_k.shape[0]
        kv_spec = pl.BlockSpec((M, W), lambda i: (0, 0))
    else:
        M = mem_k.shape[2]
        kv_spec = pl.BlockSpec((None, None, M, W), lambda i: (layer, i, 0, 0))
    in_specs = [pl.BlockSpec((bm, W), lambda i: (i + off, 0)), kv_spec, kv_spec]
    args = [qm, mem_k, mem_v]
    aliases = {}
    if not prompt:
        in_specs.append(pl.BlockSpec(memory_space=pl.ANY))
        args.append(prev_out)
        aliases[3] = 0
    return pl.pallas_call(
        functools.partial(_xattn_kernel, heads=heads, dh=dh, scale=dh ** -0.5, aliased=not prompt),
        grid=(steps,),
        in_specs=in_specs,
        out_specs=pl.BlockSpec((bm, W), lambda i: (i + off, 0)),
        out_shape=jax.ShapeDtypeStruct((R, W), BF16),
        input_output_aliases=aliases,
        compiler_params=_params("arbitrary"),
        name="xattn_prompt" if prompt else "xattn_sample",
    )(*args)


def _rope_tables(S, Bs, T, P, half):
    pos = jnp.concatenate([jnp.arange(S, dtype=jnp.int32),
                           jnp.tile(P + jnp.arange(T, dtype=jnp.int32), Bs)])
    inv_freq = ROPE_BASE ** (-jnp.arange(half, dtype=F32) / half)
    ang = pos.astype(F32)[:, None] * inv_freq[None, :]
    cos, sin = jnp.cos(ang), jnp.sin(ang)
    return jnp.concatenate([cos, cos, -sin, sin], axis=-1)


def _step(x_prompt, x_sample, cache_mla_ckv, cache_mla_krope, cache_mem_k, cache_mem_v,
          state_conv_a, state_conv_ffn, mem_prompt,
          norm_mix, w_in, q_norm, kv_norm, w_uq, w_ukv, conv_a, w_out_a, w_o_mla, w_o,
          norm_xattn, mem_norm, w_mq, w_mk, w_mv, w_mo, norm_ffn, w_up, conv_ffn, w_down,
          norm_final):
    Bp, S, D = x_prompt.shape
    Bs, T, _ = x_sample.shape
    DEPTH = w_in.shape[0]
    P = cache_mla_ckv.shape[2]
    C = cache_mla_ckv.shape[3]
    ROPE = cache_mla_krope.shape[3]
    half = ROPE // 2
    A = conv_a.shape[2]
    Q = q_norm.shape[1]
    H = w_ukv.shape[2]
    NOPE = w_uq.shape[2] // H - ROPE
    V = w_ukv.shape[3] - NOPE
    MEM = mem_prompt.shape[1]
    MH, MD = cache_mem_k.shape[3], cache_mem_k.shape[4]
    F = w_down.shape[1]
    assert Bp == 1 and NOPE == LANE and V == LANE and 4 * half == LANE
    R = S + Bs * T
    scale = (NOPE + ROPE) ** -0.5

    lay = {"A": A, "Q": Q, "C": C, "b": 0, "c": A, "h": 2 * A, "q": 3 * A, "kv": 3 * A + Q,
           "kr": 3 * A + Q + C}
    used = 3 * A + Q + C + LANE
    g0 = -(-used // 1024) * 1024
    lay["ga"], lay["gb"] = g0, g0 + D
    kr0 = 3 * A + Q + C

    x = jnp.concatenate([x_prompt.reshape(S, D), x_sample.reshape(Bs * T, D)], axis=0)
    tk = _rope_tables(S, Bs, T, P, half)
    qscale = scale * LOG2E
    tq = tk * qscale
    zeros_a = jnp.zeros((1, 2, A), F32)
    zeros_f = jnp.zeros((1, 2, 2 * F), F32)
    cache_krd = jnp.concatenate([cache_mla_krope, cache_mla_krope], axis=-1).astype(BF16)
    cmk = cache_mem_k.reshape(DEPTH, Bs, MEM, MH * MD)
    cmv = cache_mem_v.reshape(DEPTH, Bs, MEM, MH * MD)
    mem = mem_prompt.reshape(MEM, D)

    bf = lambda w: w.astype(BF16)
    w_in_t = relayout_w_in(jnp.swapaxes(w_in, 1, 2), norm_mix, kr0, half, g0, 2 * D)
    wq = w_uq.reshape(DEPTH, Q, H, NOPE + ROPE)
    x1, x2 = bf(wq[..., NOPE:NOPE + half]), bf(wq[..., NOPE + half:])
    w_uq_b = jnp.concatenate([bf(wq[..., :NOPE]), x1, x2, x2, x1], axis=-1).reshape(DEPTH, Q, H * 2 * LANE)
    w_uk_b = bf(w_ukv[..., :NOPE]).reshape(DEPTH, C, H * NOPE)
    w_uv_b = bf(w_ukv[..., NOPE:]).reshape(DEPTH, C, H * V)
    w_out_a_b, w_o_mla_b, w_o_b = bf(w_out_a), bf(w_o_mla), bf(w_o)
    w_mq_b = bf(norm_xattn[:, :, None] * w_mq)
    w_mk_b, w_mv_b, w_mo_b = bf(w_mk), bf(w_mv), bf(w_mo)
    w_up_b, w_down_b = bf(norm_ffn[:, :, None] * w_up), bf(w_down)

    outs = {k: [] for k in ("ckv_p", "kr_p", "mk", "mv", "ca_p", "cf_p", "ckv_s", "kr_s", "ca_s", "cf_s")}
    xb, rinv = row_stats(x)
    for l in range(DEPTH):
        proj = matmul_nt(xb, w_in_t, rinv, F32, l, name="matmul_in")
        po = post_in(proj, lay, zeros_a, conv_a[l], q_norm[l], kv_norm[l], tk, S, Bs, T)
        ya_in, cq, ckv, ckv_b, krc, krc_b, utail = post_in(
            proj, lay, state_conv_a[l], conv_a[l], q_norm[l], kv_norm[l], tk, S, Bs, T, prev_outs=po)
        qfull = matmul_q(cq, w_uq_b, l, tq, qscale)
        k_nope, v = matmul_kv(ckv_b, w_uk_b, w_uv_b, l, S)
        ob = prompt_attention(qfull, k_nope, krc_b, v, S, R, H)
        ob = sample_attention(qfull, w_uk_b, w_uv_b, cache_mla_ckv, cache_krd, l, ckv_b, krc_b, ob, S, Bs, T, H)
        merged = matmul_merge(ya_in, w_out_a_b, ob, w_o_mla_b, l, proj, lay["ga"], lay["gb"])
        x, xb, rinv = matmul(merged, w_o_b, F32, res=x, stats=True, layer=l, bn_target=512, name="matmul_o")

        mn = rmsnorm_rows(mem, mem_norm[l], BF16)
        mk = matmul(mn, w_mk_b, F32, layer=l, name="matmul_mk")
        mv = matmul(mn, w_mv_b, F32, layer=l, name="matmul_mv")
        qm = matmul(xb, w_mq_b, BF16, rinv=rinv, layer=l, name="matmul_mq")
        om = cross_attention(qm, mk, mv, MH, S, Bs, T)
        om = cross_attention(qm, cmk, cmv, MH, S, Bs, T, layer=l, prev_out=om)
        x, xb, rinv = matmul(om, w_mo_b, F32, res=x, stats=True, layer=l, bn_target=512, name="matmul_mo")

        up = matmul(xb, w_up_b, F32, rinv=rinv, layer=l, bn_target=512, name="matmul_up")
        hmid = ffn_gate(up, zeros_f, conv_ffn[l], S, Bs, T)
        hmid = ffn_gate(up, state_conv_ffn[l], conv_ffn[l], S, Bs, T, prev_out=hmid)
        if l + 1 < DEPTH:
            x, xb, rinv = matmul(hmid, w_down_b, F32, res=x, stats=True, layer=l, bn_target=512, k_splits=2,
                                 name="matmul_down")
        else:
            x = matmul(hmid, w_down_b, F32, res=x, layer=l, bn_target=512, k_splits=2, name="matmul_down")

        n_pt = S // _pick(S, EW_ROW_TILE, 16)
        outs["ckv_p"].append(ckv[:S].reshape(Bp, S, C))
        outs["kr_p"].append(krc[:S, :ROPE].reshape(Bp, S, ROPE))
        outs["mk"].append(mk.reshape(Bp, MEM, MH, MD))
        outs["mv"].append(mv.reshape(Bp, MEM, MH, MD))
        outs["ca_p"].append(utail[n_pt * HALO_ROWS - 2:n_pt * HALO_ROWS].reshape(Bp, 2, A))
        outs["cf_p"].append(up[S - 2:S].reshape(Bp, 2, 2 * F))
        outs["ckv_s"].append(ckv[S:].reshape(Bs, T, C))
        outs["kr_s"].append(krc[S:, :ROPE].reshape(Bs, T, ROPE))
        outs["ca_s"].append(utail[n_pt * HALO_ROWS:].reshape(Bs, HALO_ROWS, A)[:, HALO_ROWS - 2:])
        outs["cf_s"].append(up[S:].reshape(Bs, T, 2 * F)[:, T - 2:])

    y_prompt = rmsnorm_rows(x, norm_final, F32, row0=0, nrows=S).reshape(Bp, S, D)
    y_sample = rmsnorm_rows(x, norm_final, F32, row0=S, nrows=Bs * T).reshape(Bs, T, D)
    st = {k: jnp.stack(v) for k, v in outs.items()}
    return (y_prompt, y_sample, st["ckv_p"], st["kr_p"], st["mk"], st["mv"], st["ca_p"], st["cf_p"],
            st["ckv_s"], st["kr_s"], st["ca_s"], st["cf_s"])


def kernel(x_prompt, x_sample, cache_mla_ckv, cache_mla_krope, cache_mem_k, cache_mem_v, state_conv_a, state_conv_ffn, mem_prompt, norm_mix, w_in, q_norm, kv_norm, w_uq, w_ukv, conv_a, w_out_a, w_o_mla, w_o, norm_xattn, mem_norm, w_mq, w_mk, w_mv, w_mo, norm_ffn, w_up, conv_ffn, w_down, norm_final):
    return _step(x_prompt, x_sample, cache_mla_ckv, cache_mla_krope, cache_mem_k, cache_mem_v,
                 state_conv_a, state_conv_ffn, mem_prompt,
                 norm_mix, w_in, q_norm, kv_norm, w_uq, w_ukv, conv_a, w_out_a, w_o_mla, w_o,
                 norm_xattn, mem_norm, w_mq, w_mk, w_mv, w_mo, norm_ffn, w_up, conv_ffn, w_down,
                 norm_final)
```
